```python
import math
import jax, jax.numpy as jnp
from jax import lax
import numpy as np

D_MODEL = 2048
BATCH = 8
SEQ = 4096
DEPTH = 4

HEAD_DIM = 128
V_DIM = 2 * HEAD_DIM
N_HEADS = D_MODEL // (2 * HEAD_DIM)
ATTN_WIDTH = N_HEADS * V_DIM
QK_WIDTH = N_HEADS * 2 * HEAD_DIM
Q_BLOCK = 128
ROPE_THETA = 10000.0
D_RNN = D_MODEL
N_RG_BLOCKS = 8
RG_BLOCK = D_RNN // N_RG_BLOCKS
CONV_WIDTH = 4
LRU_C = 8.0
D_FF = -(-8 * D_MODEL // (3 * 256)) * 256
N_IN = 2 * QK_WIDTH + ATTN_WIDTH + 2 * D_RNN + 2 * D_MODEL
LN_EPS = 1e-5
DEEPNORM_ALPHA = (2.0 * DEPTH) ** 0.25
DEEPNORM_BETA = (8.0 * DEPTH) ** -0.25

kernel_name = "hybrid_diffattn_rglru_gated_deepnorm"


def layer_norm(x, g, b):
    xf = x.astype(jnp.float32)
    mu = jnp.mean(xf, axis=-1, keepdims=True)
    var = jnp.mean(jnp.square(xf - mu), axis=-1, keepdims=True)
    y = (xf - mu) * lax.rsqrt(var + LN_EPS) * g.astype(jnp.float32) + b.astype(jnp.float32)
    return y.astype(x.dtype)


def rope_tables(seq):
    half = HEAD_DIM // 2
    inv_freq = ROPE_THETA ** (-jnp.arange(half, dtype=jnp.float32) * 2.0 / HEAD_DIM)
    ang = jnp.arange(seq, dtype=jnp.float32)[:, None] * inv_freq[None, :]
    ang = jnp.concatenate([ang, ang], axis=-1)
    return jnp.cos(ang), jnp.sin(ang)


def apply_rope(t, cos, sin):
    half = HEAD_DIM // 2
    t1, t2 = t[..., :half], t[..., half:]
    rot = jnp.concatenate([-t2, t1], axis=-1)
    c = cos[None, :, None, None, :]
    s = sin[None, :, None, None, :]
    return t * c + rot * s


def diff_attention(q, k, v, lam):
    B, S = q.shape[0], q.shape[1]
    nb = S // Q_BLOCK
    qb = q.reshape(B, nb, Q_BLOCK, N_HEADS, 2, HEAD_DIM).transpose(1, 0, 2, 3, 4, 5)
    kpos = jnp.arange(S)
    vf = v.astype(jnp.float32)
    scale = HEAD_DIM ** -0.5

    def one_block(args):
        qblk, bi = args
        s = jnp.einsum('bqhcd,bkhcd->bhcqk', qblk, k).astype(jnp.float32) * scale
        qpos = bi * Q_BLOCK + jnp.arange(Q_BLOCK)
        causal = qpos[:, None] >= kpos[None, :]
        s = jnp.where(causal, s, -jnp.inf)
        p = jax.nn.softmax(s, axis=-1)
        a = p[:, :, 0] - lam * p[:, :, 1]
        return jnp.einsum('bhqk,bkhe->bqhe', a, vf)

    o = lax.map(one_block, (qb, jnp.arange(nb)))
    return o.transpose(1, 0, 2, 3, 4).reshape(B, S, N_HEADS, V_DIM)


def causal_depthwise_conv(x, w, b):
    C = x.shape[-1]
    y = lax.conv_general_dilated(
        x, w.astype(x.dtype)[:, None, :], window_strides=(1,),
        padding=[(CONV_WIDTH - 1, 0)], dimension_numbers=('NWC', 'WIO', 'NWC'),
        feature_group_count=C)
    return y + b.astype(x.dtype)


def rg_lru(xc, w_rg, b_rg, lru_lambda):
    B, S, _ = xc.shape
    xf = xc.astype(jnp.float32)
    xb = xf.reshape(B, S, N_RG_BLOCKS, RG_BLOCK)
    gates = jnp.einsum('bsnc,gncd->gbsnd', xb, w_rg.astype(jnp.float32)).reshape(2, B, S, D_RNN)
    gates = gates + b_rg.astype(jnp.float32)[:, None, None, :]
    r = jax.nn.sigmoid(gates[0])
    i = jax.nn.sigmoid(gates[1])
    log_a = -LRU_C * r * jax.nn.softplus(-lru_lambda.astype(jnp.float32))
    a = jnp.exp(log_a)
    mult = jnp.sqrt(-jnp.expm1(2.0 * log_a))
    u = mult * (i * xf)

    def combine(left, right):
        a1, b1 = left
        a2, b2 = right
        return a1 * a2, a2 * b1 + b2

    _, h = lax.associative_scan(combine, (a, u), axis=1)
    return h


def setup_inputs(seed: int = 0) -> dict:
    key = jax.random.key(seed)
    ks = jax.random.split(key, 18)
    f32 = jnp.float32
    x = jax.random.normal(ks[0], (BATCH, SEQ, D_MODEL), f32)
    w_in = jax.random.normal(ks[1], (DEPTH, D_MODEL, N_IN), f32) * D_MODEL ** -0.5
    b_merge = 0.1 * jax.random.normal(ks[2], (DEPTH, 2, D_MODEL), f32)
    diff_lambda = 0.1 * jax.random.normal(ks[3], (DEPTH, 4, HEAD_DIM), f32)
    subln_g = 1.0 + 0.02 * jax.random.normal(ks[4], (DEPTH, V_DIM), f32)
    conv_w = jax.random.normal(ks[5], (DEPTH, CONV_WIDTH, D_RNN), f32) * CONV_WIDTH ** -0.5
    conv_b = 0.02 * jax.random.normal(ks[6], (DEPTH, D_RNN), f32)
    w_rg = jax.random.normal(ks[7], (DEPTH, 2, N_RG_BLOCKS, RG_BLOCK, RG_BLOCK), f32) * RG_BLOCK ** -0.5
    b_rg = 0.1 * jax.random.normal(ks[8], (DEPTH, 2, D_RNN), f32)
    u = jax.random.uniform(ks[9], (DEPTH, D_RNN), f32, minval=0.9, maxval=0.999)
    a0 = u ** (1.0 / LRU_C)
    lru_lambda = jnp.log(a0) - jnp.log1p(-a0)
    w_branch = jax.random.normal(ks[10], (DEPTH, 2, ATTN_WIDTH, D_MODEL), f32) * (ATTN_WIDTH ** -0.5 * DEEPNORM_BETA)
    w_out = jax.random.normal(ks[11], (DEPTH, D_MODEL, D_MODEL), f32) * (D_MODEL ** -0.5 * DEEPNORM_BETA)
    ln_g = 1.0 + 0.02 * jax.random.normal(ks[12], (DEPTH, 2, D_MODEL), f32)
    ln_b = 0.02 * jax.random.normal(ks[13], (DEPTH, 2, D_MODEL), f32)
    w_gate = jax.random.normal(ks[14], (DEPTH, D_MODEL, D_FF), f32) * D_MODEL ** -0.5
    w_up = jax.random.normal(ks[15], (DEPTH, D_MODEL, D_FF), f32) * (D_MODEL ** -0.5 * DEEPNORM_BETA)
    w_gate_up = jnp.concatenate([w_gate, w_up], axis=-1)
    w_down = jax.random.normal(ks[16], (DEPTH, D_FF, D_MODEL), f32) * (D_FF ** -0.5 * DEEPNORM_BETA)
    return {"x": x, "w_in": w_in, "b_merge": b_merge, "diff_lambda": diff_lambda,
            "subln_g": subln_g, "conv_w": conv_w, "conv_b": conv_b, "w_rg": w_rg,
            "b_rg": b_rg, "lru_lambda": lru_lambda, "w_branch": w_branch, "w_out": w_out,
            "ln_g": ln_g, "ln_b": ln_b, "w_gate_up": w_gate_up, "w_down": w_down}


def reference(x, w_in, b_merge, diff_lambda, subln_g, conv_w, conv_b, w_rg, b_rg,
              lru_lambda, w_branch, w_out, ln_g, ln_b, w_gate_up, w_down):
    B, S, D = x.shape
    cos, sin = rope_tables(S)
    o1 = 2 * QK_WIDTH
    o2 = o1 + ATTN_WIDTH
    o3 = o2 + D_RNN
    o4 = o3 + D_RNN
    for l in range(DEPTH):
        lam_init = 0.8 - 0.6 * math.exp(-0.3 * l)
        p = jnp.einsum('bsd,dn->bsn', x, w_in[l])
        q = p[..., :QK_WIDTH].reshape(B, S, N_HEADS, 2, HEAD_DIM)
        k = p[..., QK_WIDTH:o1].reshape(B, S, N_HEADS, 2, HEAD_DIM)
        v = p[..., o1:o2].reshape(B, S, N_HEADS, V_DIM)
        xr = p[..., o2:o3]
        gr = p[..., o3:o4]
        gm = p[..., o4:].reshape(B, S, 2, D)

        q = apply_rope(q, cos, sin)
        k = apply_rope(k, cos, sin)
        lv = diff_lambda[l].astype(jnp.float32)
        lam = jnp.exp(jnp.sum(lv[0] * lv[1])) - jnp.exp(jnp.sum(lv[2] * lv[3])) + lam_init
        o = diff_attention(q, k, v, lam)
        o = o * lax.rsqrt(jnp.mean(jnp.square(o), axis=-1, keepdims=True) + LN_EPS)
        o = (o * subln_g[l].astype(jnp.float32) * (1.0 - lam_init)).reshape(B, S, ATTN_WIDTH)
        y_attn = jnp.einsum('bse,ed->bsd', o.astype(x.dtype), w_branch[l, 0])

        xc = causal_depthwise_conv(xr, conv_w[l], conv_b[l])
        h = rg_lru(xc, w_rg[l], b_rg[l], lru_lambda[l])
        hr = (h * jax.nn.gelu(gr.astype(jnp.float32))).astype(x.dtype)
        y_rnn = jnp.einsum('bse,ed->bsd', hr, w_branch[l, 1])

        g = jax.nn.sigmoid(gm + b_merge[l])
        merged = g[:, :, 0, :] * y_attn + g[:, :, 1, :] * y_rnn
        mix = jnp.einsum('bsd,de->bse', merged, w_out[l])
        x = layer_norm(DEEPNORM_ALPHA * x + mix, ln_g[l, 0], ln_b[l, 0])

        hu = jnp.einsum('bsd,df->bsf', x, w_gate_up[l])
        act = jax.nn.silu(hu[..., :D_FF]) * hu[..., D_FF:]
        ffn = jnp.einsum('bsf,fd->bsd', act, w_down[l])
        x = layer_norm(DEEPNORM_ALPHA * x + ffn, ln_g[l, 1], ln_b[l, 1])
    return x
```

```python
import functools
import math

import jax
import jax.numpy as jnp
from jax import lax
from jax.experimental import pallas as pl
from jax.experimental.pallas import tpu as pltpu

HEAD_DIM = 128
ROPE_THETA = 10000.0
CONV_WIDTH = 4
LRU_C = 8.0
LN_EPS = 1e-5

LANES = 128
SUBLANES = 8
V7X_VMEM_BYTES = 64 * 1024 * 1024
V7X_VMEM_CEILING = V7X_VMEM_BYTES - 6 * 1024 * 1024

BF16 = jnp.bfloat16
F32 = jnp.float32


def _vmem_limit(pipelined_bytes, resident_bytes=0, temp_bytes=0):
    need = 2 * pipelined_bytes + resident_bytes + temp_bytes + (4 << 20)
    return int(min(max(need, 16 << 20), V7X_VMEM_CEILING))


def _nbytes(shape, dtype):
    return math.prod(shape) * jnp.dtype(dtype).itemsize


def _tile(dim, want):
    t = min(dim, want)
    while dim % t:
        t //= 2
    return t


def _proj_kernel(x_ref, w_ref, o_ref):
    acc = jnp.dot(x_ref[...], w_ref[...], preferred_element_type=F32)
    o_ref[...] = acc.astype(o_ref.dtype)


def _proj_rope_kernel(x_ref, w_ref, cos_ref, sin_ref, o_ref):
    acc = jnp.dot(x_ref[...], w_ref[...], preferred_element_type=F32)
    cos = cos_ref[...]
    sin = sin_ref[...]
    for c in range(acc.shape[1] // HEAD_DIM):
        t = acc[:, c * HEAD_DIM:(c + 1) * HEAD_DIM]
        rot = pltpu.roll(t, HEAD_DIM // 2, axis=1)
        o_ref[:, c * HEAD_DIM:(c + 1) * HEAD_DIM] = (t * cos + rot * sin).astype(o_ref.dtype)


def _project(x, w_stack, layer, col0, ncols, out_dtype, *, tm, tn, rope=None, seq=None):
    T, K = x.shape
    tm = _tile(T, tm)
    tn = _tile(ncols, tn)
    assert col0 % tn == 0
    cb0 = col0 // tn
    grid = (T // tm, ncols // tn)
    in_specs = [
        pl.BlockSpec((tm, K), lambda i, j: (i, 0)),
        pl.BlockSpec((None, K, tn), lambda i, j: (layer, 0, j + cb0)),
    ]
    args = [x, w_stack]
    blocks = _nbytes((tm, K), BF16) + _nbytes((K, tn), BF16) + _nbytes((tm, tn), out_dtype)
    if rope is None:
        body = _proj_kernel
    else:
        cos_t, sin_t = rope
        groups = cos_t.shape[0]
        per = (ncols // tn) // groups
        tms = _tile(seq, tm)
        assert tms == tm
        nsb = seq // tm
        tab_spec = pl.BlockSpec((None, tm, HEAD_DIM), lambda i, j: (j // per, i % nsb, 0))
        in_specs += [tab_spec, tab_spec]
        args += [cos_t, sin_t]
        blocks += 2 * _nbytes((tm, HEAD_DIM), F32)
        body = _proj_rope_kernel
    return pl.pallas_call(
        body,
        name=f"proj_c{col0}_l{layer}",
        grid=grid,
        in_specs=in_specs,
        out_specs=pl.BlockSpec((tm, tn), lambda i, j: (i, j)),
        out_shape=jax.ShapeDtypeStruct((T, ncols), out_dtype),
        compiler_params=pltpu.CompilerParams(
            dimension_semantics=("parallel", "arbitrary"),
            vmem_limit_bytes=_vmem_limit(blocks, temp_bytes=2 * _nbytes((tm, tn), F32))),
    )(*args)


def _attn_kernel(lam_ref, g_ref, q_ref, k_ref, v_ref, o_ref, m_ref, l_ref, acc_ref, *, tq, lam_init):
    qi = pl.program_id(2)
    m_ref[...] = jnp.full(m_ref.shape, -jnp.inf, F32)
    l_ref[...] = jnp.zeros(l_ref.shape, F32)
    acc_ref[...] = jnp.zeros(acc_ref.shape, F32)

    def step(kj, masked):
        start = pl.multiple_of(kj * tq, tq)
        kblk = k_ref[pl.ds(start, tq), :]
        vblk = v_ref[pl.ds(start, tq), :]
        for c in range(2):
            qc = q_ref[:, c * HEAD_DIM:(c + 1) * HEAD_DIM]
            kc = kblk[:, c * HEAD_DIM:(c + 1) * HEAD_DIM]
            s = lax.dot_general(qc, kc, (((1,), (1,)), ((), ())), preferred_element_type=F32)
            if masked:
                row = lax.broadcasted_iota(jnp.int32, s.shape, 0)
                col = lax.broadcasted_iota(jnp.int32, s.shape, 1)
                s = jnp.where(row >= col, s, -jnp.inf)
            m_old = m_ref[c]
            m_new = jnp.maximum(m_old, jnp.max(s, axis=-1, keepdims=True))
            alpha = jnp.exp(m_old - m_new)
            p = jnp.exp(s - m_new)
            l_ref[c] = alpha * l_ref[c] + jnp.sum(p, axis=-1, keepdims=True)
            acc_ref[c] = alpha * acc_ref[c] + jnp.dot(p.astype(BF16), vblk, preferred_element_type=F32)
            m_ref[c] = m_new

    def full_block(kj, carry):
        step(kj, False)
        return carry

    lax.fori_loop(0, qi, full_block, 0)
    step(qi, True)

    lv = lam_ref[...]
    lam = (jnp.exp(jnp.sum(lv[0:1] * lv[1:2], axis=-1, keepdims=True))
           - jnp.exp(jnp.sum(lv[2:3] * lv[3:4], axis=-1, keepdims=True)) + lam_init)
    o = acc_ref[0] * (1.0 / l_ref[0]) - lam * (acc_ref[1] * (1.0 / l_ref[1]))
    ms = jnp.mean(o * o, axis=-1, keepdims=True)
    o = o * lax.rsqrt(ms + LN_EPS)
    o_ref[...] = (o * g_ref[...] * (1.0 - lam_init)).astype(o_ref.dtype)


def _diff_attention(qk, v, lam_params, subln_g, layer, lam_init, *, batch, seq, tq):
    T, W = v.shape
    hw = 2 * HEAD_DIM
    heads = W // hw
    tq = _tile(seq, tq)
    nq = seq // tq
    blocks = 2 * _nbytes((tq, hw), BF16) + 2 * _nbytes((seq, hw), BF16)
    scratch = 2 * _nbytes((tq, hw), F32) + 4 * _nbytes((tq, LANES), F32)
    return pl.pallas_call(
        functools.partial(_attn_kernel, tq=tq, lam_init=lam_init),
        name=f"diff_attn_l{layer}",
        grid=(batch, heads, nq),
        in_specs=[
            pl.BlockSpec((None, 4, HEAD_DIM), lambda b, h, i: (layer, 0, 0)),
            pl.BlockSpec((None, 1, hw), lambda b, h, i: (layer, 0, 0)),
            pl.BlockSpec((tq, hw), lambda b, h, i: (b * nq + i, h)),
            pl.BlockSpec((seq, hw), lambda b, h, i: (b, heads + h)),
            pl.BlockSpec((seq, hw), lambda b, h, i: (b, h)),
        ],
        out_specs=pl.BlockSpec((tq, hw), lambda b, h, i: (b * nq + i, h)),
        out_shape=jax.ShapeDtypeStruct((T, W), BF16),
        scratch_shapes=[
            pltpu.VMEM((2, tq, 1), F32),
            pltpu.VMEM((2, tq, 1), F32),
            pltpu.VMEM((2, tq, hw), F32),
        ],
        compiler_params=pltpu.CompilerParams(
            dimension_semantics=("parallel", "parallel", "arbitrary"),
            vmem_limit_bytes=_vmem_limit(blocks, scratch, 6 * _nbytes((tq, tq), F32))),
    )(lam_params, subln_g, qk, qk, v)


def _rnn_kernel(xr_ref, gr_ref, cw_ref, cb_ref, wrg_ref, brg_ref, lam_ref, o_ref,
                xbuf, a_s, b_s, h_s, hcar, *, tt, nblk):
    t = pl.program_id(1)
    C = xr_ref.shape[1]
    bw = C // nblk

    @pl.when(t == 0)
    def _():
        xbuf[0:SUBLANES, :] = jnp.zeros((SUBLANES, C), F32)
        hcar[...] = jnp.zeros(hcar.shape, F32)

    x = xr_ref[...]
    xbuf[SUBLANES:SUBLANES + tt, :] = x
    cw = cw_ref[...]
    xc = cw[CONV_WIDTH - 1:CONV_WIDTH] * x + cb_ref[...]
    for d in range(1, CONV_WIDTH):
        xc = xc + cw[CONV_WIDTH - 1 - d:CONV_WIDTH - d] * xbuf[SUBLANES - d:SUBLANES - d + tt, :]
    xbuf[0:SUBLANES, :] = x[tt - SUBLANES:tt, :]

    sub = lax.broadcasted_iota(jnp.int32, (tt // SUBLANES, SUBLANES, bw), 1)
    for n in range(nblk):
        cs = slice(n * bw, (n + 1) * bw)
        xcn = xc[:, cs]
        xb = xcn.astype(BF16)
        gr_pre = jnp.dot(xb, wrg_ref[0, n], preferred_element_type=F32) + brg_ref[0:1, cs]
        gi_pre = jnp.dot(xb, wrg_ref[1, n], preferred_element_type=F32) + brg_ref[1:2, cs]
        r = jax.nn.sigmoid(gr_pre)
        i = jax.nn.sigmoid(gi_pre)
        log_a = (-LRU_C) * r * jax.nn.softplus(-lam_ref[:, cs])
        a = jnp.exp(log_a)
        u = jnp.sqrt((1.0 - a) * (1.0 + a)) * (i * xcn)
        a3 = a.reshape(tt // SUBLANES, SUBLANES, bw)
        u3 = u.reshape(tt // SUBLANES, SUBLANES, bw)
        for s in (1, 2, 4):
            keep = sub >= s
            u_new = a3 * pltpu.roll(u3, s, axis=1) + u3
            a_new = a3 * pltpu.roll(a3, s, axis=1)
            u3 = jnp.where(keep, u_new, u3)
            a3 = jnp.where(keep, a_new, a3)
        a_s[:, cs] = a3.reshape(tt, bw)
        b_s[:, cs] = u3.reshape(tt, bw)

    def group(j, h):
        rows = pl.ds(pl.multiple_of(j * SUBLANES, SUBLANES), SUBLANES)
        hj = a_s[rows, :] * h + b_s[rows, :]
        h_s[rows, :] = hj
        return jnp.broadcast_to(hj[SUBLANES - 1:SUBLANES, :], hj.shape)

    hcar[...] = lax.fori_loop(0, tt // SUBLANES, group, hcar[...])
    o_ref[...] = (h_s[...] * jax.nn.gelu(gr_ref[...])).astype(o_ref.dtype)


def _rglru(rest, conv_w, conv_b, w_rg, b_rg, lru_lambda, layer, *, batch, seq, width, tt):
    T = rest.shape[0]
    nblk = w_rg.shape[2]
    tt = _tile(seq, tt)
    nt = seq // tt
    blocks = 2 * _nbytes((tt, width), F32) + _nbytes((tt, width), BF16) + _nbytes(w_rg.shape[1:], BF16)
    scratch = 4 * _nbytes((tt + SUBLANES, width), F32)
    return pl.pallas_call(
        functools.partial(_rnn_kernel, tt=tt, nblk=nblk),
        name=f"rglru_l{layer}",
        grid=(batch, nt),
        in_specs=[
            pl.BlockSpec((tt, width), lambda b, t: (b * nt + t, 0)),
            pl.BlockSpec((tt, width), lambda b, t: (b * nt + t, 1)),
            pl.BlockSpec((None, CONV_WIDTH, width), lambda b, t: (layer, 0, 0)),
            pl.BlockSpec((None, 1, width), lambda b, t: (layer, 0, 0)),
            pl.BlockSpec((None,) + w_rg.shape[1:], lambda b, t: (layer, 0, 0, 0, 0)),
            pl.BlockSpec((None, 2, width), lambda b, t: (layer, 0, 0)),
            pl.BlockSpec((None, 1, width), lambda b, t: (layer, 0, 0)),
        ],
        out_specs=pl.BlockSpec((tt, width), lambda b, t: (b * nt + t, 0)),
        out_shape=jax.ShapeDtypeStruct((T, width), BF16),
        scratch_shapes=[
            pltpu.VMEM((tt + SUBLANES, width), F32),
            pltpu.VMEM((tt, width), F32),
            pltpu.VMEM((tt, width), F32),
            pltpu.VMEM((tt, width), F32),
            pltpu.VMEM((SUBLANES, width), F32),
        ],
        compiler_params=pltpu.CompilerParams(
            dimension_semantics=("parallel", "arbitrary"),
            vmem_limit_bytes=_vmem_limit(blocks, scratch, 6 * _nbytes((tt, width), F32))),
    )(rest, rest, conv_w, conv_b, w_rg, b_rg, lru_lambda)


def _merge_kernel(oa_ref, hr_ref, wa_ref, wr_ref, ga_ref, gr_ref, bm_ref, o_ref):
    ya = jnp.dot(oa_ref[...], wa_ref[...], preferred_element_type=F32)
    yr = jnp.dot(hr_ref[...], wr_ref[...], preferred_element_type=F32)
    ga = jax.nn.sigmoid(ga_ref[...] + bm_ref[0:1, :])
    gr = jax.nn.sigmoid(gr_ref[...] + bm_ref[1:2, :])
    o_ref[...] = (ga * ya + gr * yr).astype(o_ref.dtype)


def _merge(o_attn, h_rnn, w_branch, rest, b_merge, layer, *, tm, tn):
    T, K = o_attn.shape
    D = w_branch.shape[-1]
    tm = _tile(T, tm)
    tn = _tile(D, tn)
    nb = D // tn
    gate0 = (rest.shape[1] - 2 * D) // tn
    blocks = (2 * _nbytes((tm, K), BF16) + 2 * _nbytes((K, tn), BF16) + 2 * _nbytes((tm, tn), F32)
              + _nbytes((tm, tn), BF16))
    return pl.pallas_call(
        _merge_kernel,
        name=f"merge_l{layer}",
        grid=(T // tm, nb),
        in_specs=[
            pl.BlockSpec((tm, K), lambda i, j: (i, 0)),
            pl.BlockSpec((tm, K), lambda i, j: (i, 0)),
            pl.BlockSpec((None, None, K, tn), lambda i, j: (layer, 0, 0, j)),
            pl.BlockSpec((None, None, K, tn), lambda i, j: (layer, 1, 0, j)),
            pl.BlockSpec((tm, tn), lambda i, j: (i, gate0 + j)),
            pl.BlockSpec((tm, tn), lambda i, j: (i, gate0 + nb + j)),
            pl.BlockSpec((None, 2, tn), lambda i, j: (layer, 0, j)),
        ],
        out_specs=pl.BlockSpec((tm, tn), lambda i, j: (i, j)),
        out_shape=jax.ShapeDtypeStruct((T, D), BF16),
        compiler_params=pltpu.CompilerParams(
            dimension_semantics=("parallel", "arbitrary"),
            vmem_limit_bytes=_vmem_limit(blocks, temp_bytes=4 * _nbytes((tm, tn), F32))),
    )(o_attn, h_rnn, w_branch, w_branch, rest, rest, b_merge)


def _layer_norm_rows(xf, g, b):
    mu = jnp.mean(xf, axis=-1, keepdims=True)
    d = xf - mu
    var = jnp.mean(d * d, axis=-1, keepdims=True)
    return d * lax.rsqrt(var + LN_EPS) * g + b


def _outproj_ln_kernel(m_ref, w_ref, x_ref, g_ref, b_ref, of_ref, ob_ref, *, alpha, which):
    mix = jnp.dot(m_ref[...], w_ref[...], preferred_element_type=F32)
    y = _layer_norm_rows(alpha * x_ref[...] + mix, g_ref[which:which + 1, :], b_ref[which:which + 1, :])
    of_ref[...] = y
    ob_ref[...] = y.astype(BF16)


def _outproj_ln(merged, w_out, x, ln_g, ln_b, layer, alpha, *, tm):
    T, K = merged.shape
    D = w_out.shape[-1]
    tm = _tile(T, tm)
    blocks = (_nbytes((tm, K), BF16) + _nbytes((K, D), BF16) + 2 * _nbytes((tm, D), F32)
              + _nbytes((tm, D), BF16))
    return pl.pallas_call(
        functools.partial(_outproj_ln_kernel, alpha=alpha, which=0),
        name=f"outproj_ln_l{layer}",
        grid=(T // tm,),
        in_specs=[
            pl.BlockSpec((tm, K), lambda i: (i, 0)),
            pl.BlockSpec((None, K, D), lambda i: (layer, 0, 0)),
            pl.BlockSpec((tm, D), lambda i: (i, 0)),
            pl.BlockSpec((None, 2, D), lambda i: (layer, 0, 0)),
            pl.BlockSpec((None, 2, D), lambda i: (layer, 0, 0)),
        ],
        out_specs=[pl.BlockSpec((tm, D), lambda i: (i, 0)), pl.BlockSpec((tm, D), lambda i: (i, 0))],
        out_shape=[jax.ShapeDtypeStruct((T, D), F32), jax.ShapeDtypeStruct((T, D), BF16)],
        compiler_params=pltpu.CompilerParams(
            dimension_semantics=("parallel",),
            vmem_limit_bytes=_vmem_limit(blocks, temp_bytes=3 * _nbytes((tm, D), F32))),
    )(merged, w_out, x, ln_g, ln_b)


def _ffn_ln_kernel(xb_ref, wg_ref, wu_ref, wd_ref, x_ref, g_ref, b_ref, of_ref, ob_ref, acc_ref, *, alpha):
    f = pl.program_id(1)
    xb = xb_ref[...]
    hg = jnp.dot(xb, wg_ref[...], preferred_element_type=F32)
    hu = jnp.dot(xb, wu_ref[...], preferred_element_type=F32)
    act = (jax.nn.silu(hg) * hu).astype(BF16)
    part = jnp.dot(act, wd_ref[...], preferred_element_type=F32)

    @pl.when(f == 0)
    def _():
        acc_ref[...] = part

    @pl.when(f > 0)
    def _():
        acc_ref[...] += part

    @pl.when(f == pl.num_programs(1) - 1)
    def _():
        y = _layer_norm_rows(alpha * x_ref[...] + acc_ref[...], g_ref[1:2, :], b_ref[1:2, :])
        of_ref[...] = y
        ob_ref[...] = y.astype(BF16)


def _ffn_ln(xb, x, w_gate_up, w_down, ln_g, ln_b, layer, alpha, *, tm, tf):
    T, D = x.shape
    F = w_down.shape[1]
    tm = _tile(T, tm)
    tf = _tile(F, tf)
    nf = F // tf
    blocks = (_nbytes((tm, D), BF16) + 3 * _nbytes((D, tf), BF16) + 2 * _nbytes((tm, D), F32)
              + _nbytes((tm, D), BF16))
    return pl.pallas_call(
        functools.partial(_ffn_ln_kernel, alpha=alpha),
        name=f"ffn_ln_l{layer}",
        grid=(T // tm, nf),
        in_specs=[
            pl.BlockSpec((tm, D), lambda i, f: (i, 0)),
            pl.BlockSpec((None, D, tf), lambda i, f: (layer, 0, f)),
            pl.BlockSpec((None, D, tf), lambda i, f: (layer, 0, nf + f)),
            pl.BlockSpec((None, tf, D), lambda i, f: (layer, f, 0)),
            pl.BlockSpec((tm, D), lambda i, f: (i, 0)),
            pl.BlockSpec((None, 2, D), lambda i, f: (layer, 0, 0)),
            pl.BlockSpec((None, 2, D), lambda i, f: (layer, 0, 0)),
        ],
        out_specs=[pl.BlockSpec((tm, D), lambda i, f: (i, 0)), pl.BlockSpec((tm, D), lambda i, f: (i, 0))],
        out_shape=[jax.ShapeDtypeStruct((T, D), F32), jax.ShapeDtypeStruct((T, D), BF16)],
        scratch_shapes=[pltpu.VMEM((tm, D), F32)],
        compiler_params=pltpu.CompilerParams(
            dimension_semantics=("parallel", "arbitrary"),
            vmem_limit_bytes=_vmem_limit(blocks, _nbytes((tm, D), F32),
                                         3 * _nbytes((tm, tf), F32) + _nbytes((tm, D), F32))),
    )(xb, w_gate_up, w_gate_up, w_down, x, ln_g, ln_b)


def _rope_tables(seq, q_scale):
    half = HEAD_DIM // 2
    inv_freq = ROPE_THETA ** (-jnp.arange(half, dtype=F32) * 2.0 / HEAD_DIM)
    ang = jnp.arange(seq, dtype=F32)[:, None] * inv_freq[None, :]
    ang = jnp.concatenate([ang, ang], axis=-1)
    sign = jnp.concatenate([-jnp.ones((half,), F32), jnp.ones((half,), F32)])
    cos = jnp.cos(ang)
    sin = jnp.sin(ang) * sign
    return jnp.stack([cos * q_scale, cos]), jnp.stack([sin * q_scale, sin])


def kernel(x, w_in, b_merge, diff_lambda, subln_g, conv_w, conv_b, w_rg, b_rg, lru_lambda,
           w_branch, w_out, ln_g, ln_b, w_gate_up, w_down):
    B, S, D = x.shape
    T = B * S
    depth = w_in.shape[0]
    attn_w = w_branch.shape[2]
    d_rnn = conv_w.shape[-1]
    qk_w = (w_in.shape[-1] - attn_w - 2 * d_rnn - 2 * D) // 2
    alpha = (2.0 * depth) ** 0.25

    w_in_b = w_in.astype(BF16)
    w_rg_b = w_rg.astype(BF16)
    w_branch_b = w_branch.astype(BF16)
    w_out_b = w_out.astype(BF16)
    w_gate_up_b = w_gate_up.astype(BF16)
    w_down_b = w_down.astype(BF16)
    subln_g3 = subln_g.reshape(depth, 1, -1)
    conv_b3 = conv_b.reshape(depth, 1, -1)
    lru_lambda3 = lru_lambda.reshape(depth, 1, -1)
    cos_t, sin_t = _rope_tables(S, HEAD_DIM ** -0.5)

    xf = x.reshape(T, D)
    xb = xf.astype(BF16)
    for l in range(depth):
        lam_init = 0.8 - 0.6 * math.exp(-0.3 * l)
        qk = _project(xb, w_in_b, l, 0, 2 * qk_w, BF16, tm=1024, tn=1024, rope=(cos_t, sin_t), seq=S)
        v = _project(xb, w_in_b, l, 2 * qk_w, attn_w, BF16, tm=1024, tn=1024)
        rest = _project(xb, w_in_b, l, 2 * qk_w + attn_w, 2 * d_rnn + 2 * D, F32, tm=1024, tn=1024)
        o_attn = _diff_attention(qk, v, diff_lambda, subln_g3, l, lam_init,
                                 batch=B, seq=S, tq=512)
        h_rnn = _rglru(rest, conv_w, conv_b3, w_rg_b, b_rg, lru_lambda3, l,
                       batch=B, seq=S, width=d_rnn, tt=512)
        merged = _merge(o_attn, h_rnn, w_branch_b, rest, b_merge, l, tm=1024, tn=512)
        xf, xb = _outproj_ln(merged, w_out_b, xf, ln_g, ln_b, l, alpha, tm=512)
        xf, xb = _ffn_ln(xb, xf, w_gate_up_b, w_down_b, ln_g, ln_b, l, alpha, tm=512, tf=512)
    return xf.reshape(B, S, D)
```

```python
import functools
import math

import jax
import jax.numpy as jnp
from jax import lax
from jax.experimental import pallas as pl
from jax.experimental.pallas import tpu as pltpu

HEAD_DIM = 128
ROPE_THETA = 10000.0
CONV_WIDTH = 4
LRU_C = 8.0
LN_EPS = 1e-5

LANES = 128
SUBLANES = 8
V7X_VMEM_BYTES = 64 * 1024 * 1024
V7X_VMEM_CEILING = V7X_VMEM_BYTES - 6 * 1024 * 1024

BF16 = jnp.bfloat16
F32 = jnp.float32


def _vmem_limit(pipelined_bytes, resident_bytes=0, temp_bytes=0):
    need = 2 * pipelined_bytes + resident_bytes + temp_bytes + (4 << 20)
    return int(min(max(need, 16 << 20), V7X_VMEM_CEILING))


def _nbytes(shape, dtype):
    return math.prod(shape) * jnp.dtype(dtype).itemsize


def _tile(dim, want):
    t = min(dim, want)
    while dim % t:
        t //= 2
    return t


def _proj_kernel(x_ref, w_ref, o_ref):
    acc = jnp.dot(x_ref[...], w_ref[...], preferred_element_type=F32)
    o_ref[...] = acc.astype(o_ref.dtype)


def _proj_rope_kernel(x_ref, w_ref, cos_ref, sin_ref, o_ref):
    acc = jnp.dot(x_ref[...], w_ref[...], preferred_element_type=F32)
    cos = cos_ref[...]
    sin = sin_ref[...]
    for c in range(acc.shape[1] // HEAD_DIM):
        t = acc[:, c * HEAD_DIM:(c + 1) * HEAD_DIM]
        rot = pltpu.roll(t, HEAD_DIM // 2, axis=1)
        o_ref[:, c * HEAD_DIM:(c + 1) * HEAD_DIM] = (t * cos + rot * sin).astype(o_ref.dtype)


def _project(x, w_stack, layer, col0, ncols, out_dtype, *, tm, tn, rope=None, seq=None):
    T, K = x.shape
    tm = _tile(T, tm)
    tn = _tile(ncols, tn)
    assert col0 % tn == 0
    cb0 = col0 // tn
    grid = (T // tm, ncols // tn)
    in_specs = [
        pl.BlockSpec((tm, K), lambda i, j: (i, 0)),
        pl.BlockSpec((None, K, tn), lambda i, j: (layer, 0, j + cb0)),
    ]
    args = [x, w_stack]
    blocks = _nbytes((tm, K), BF16) + _nbytes((K, tn), BF16) + _nbytes((tm, tn), out_dtype)
    if rope is None:
        body = _proj_kernel
    else:
        cos_t, sin_t = rope
        groups = cos_t.shape[0]
        per = (ncols // tn) // groups
        tms = _tile(seq, tm)
        assert tms == tm
        nsb = seq // tm
        tab_spec = pl.BlockSpec((None, tm, HEAD_DIM), lambda i, j: (j // per, i % nsb, 0))
        in_specs += [tab_spec, tab_spec]
        args += [cos_t, sin_t]
        blocks += 2 * _nbytes((tm, HEAD_DIM), F32)
        body = _proj_rope_kernel
    return pl.pallas_call(
        body,
        name=f"proj_c{col0}_l{layer}",
        grid=grid,
        in_specs=in_specs,
        out_specs=pl.BlockSpec((tm, tn), lambda i, j: (i, j)),
        out_shape=jax.ShapeDtypeStruct((T, ncols), out_dtype),
        compiler_params=pltpu.CompilerParams(
            dimension_semantics=("parallel", "arbitrary"),
            vmem_limit_bytes=_vmem_limit(blocks, temp_bytes=2 * _nbytes((tm, tn), F32))),
    )(*args)


_NT = (((1,), (1,)), ((), ()))


def _proj_t_kernel(w_ref, x_ref, o_ref):
    acc = lax.dot_general(w_ref[...], x_ref[...], _NT, preferred_element_type=F32)
    o_ref[...] = acc.astype(o_ref.dtype)


def _proj_t_rope_kernel(w_ref, x_ref, cos_ref, sin_ref, o_ref):
    acc = lax.dot_general(w_ref[...], x_ref[...], _NT, preferred_element_type=F32)
    cos = cos_ref[...]
    sin = sin_ref[...]
    half = HEAD_DIM // 2
    for c in range(acc.shape[0] // HEAD_DIM):
        t = acc[c * HEAD_DIM:(c + 1) * HEAD_DIM, :]
        rot = jnp.concatenate([t[half:], t[:half]], axis=0)
        o_ref[c * HEAD_DIM:(c + 1) * HEAD_DIM, :] = (t * cos + rot * sin).astype(o_ref.dtype)


def _project_t(x, wt_stack, layer, name, *, tm, tn, rope=None, seq=None):
    T, K = x.shape
    N = wt_stack.shape[1]
    tm = _tile(T, tm)
    tn = _tile(N, tn)
    in_specs = [
        pl.BlockSpec((None, tn, K), lambda i, j: (layer, j, 0)),
        pl.BlockSpec((tm, K), lambda i, j: (i, 0)),
    ]
    args = [wt_stack, x]
    blocks = _nbytes((tm, K), BF16) + _nbytes((K, tn), BF16) + _nbytes((tm, tn), BF16)
    if rope is None:
        body = _proj_t_kernel
    else:
        assert seq % tm == 0
        nsb = seq // tm
        tab_spec = pl.BlockSpec((HEAD_DIM, tm), lambda i, j: (0, i % nsb))
        in_specs += [tab_spec, tab_spec]
        args += list(rope)
        blocks += 2 * _nbytes((tm, HEAD_DIM), F32)
        body = _proj_t_rope_kernel
    return pl.pallas_call(
        body,
        name=f"{name}_l{layer}",
        grid=(T // tm, N // tn),
        in_specs=in_specs,
        out_specs=pl.BlockSpec((tn, tm), lambda i, j: (j, i)),
        out_shape=jax.ShapeDtypeStruct((N, T), BF16),
        compiler_params=pltpu.CompilerParams(
            dimension_semantics=("parallel", "arbitrary"),
            vmem_limit_bytes=_vmem_limit(blocks, temp_bytes=2 * _nbytes((tm, tn), F32))),
    )(*args)


def _attn_kernel(lam_ref, g_ref, qt_ref, k_ref, vt_ref, o_ref, m_ref, l_ref, acc_ref, *, tq, lam_init):
    qi = pl.program_id(2)
    m_ref[...] = jnp.full(m_ref.shape, -jnp.inf, F32)
    l_ref[...] = jnp.zeros(l_ref.shape, F32)
    acc_ref[...] = jnp.zeros(acc_ref.shape, F32)

    def step(kj, masked):
        start = pl.multiple_of(kj * tq, tq)
        kblk = k_ref[pl.ds(start, tq), :]
        vtblk = vt_ref[:, pl.ds(start, tq)]
        for c in range(2):
            kc = kblk[:, c * HEAD_DIM:(c + 1) * HEAD_DIM]
            qtc = qt_ref[c * HEAD_DIM:(c + 1) * HEAD_DIM, :]
            st = jnp.dot(kc, qtc, preferred_element_type=F32)
            if masked:
                key = lax.broadcasted_iota(jnp.int32, st.shape, 0)
                qry = lax.broadcasted_iota(jnp.int32, st.shape, 1)
                st = jnp.where(qry >= key, st, -jnp.inf)
            m_old = m_ref[c]
            m_new = jnp.maximum(m_old, jnp.max(st, axis=0, keepdims=True))
            alpha = jnp.exp2(m_old - m_new)
            pt = jnp.exp2(st - m_new)
            l_ref[c] = alpha * l_ref[c] + jnp.sum(pt, axis=0, keepdims=True)
            acc_ref[c] = alpha * acc_ref[c] + jnp.dot(vtblk, pt.astype(BF16), preferred_element_type=F32)
            m_ref[c] = m_new

    def full_block(kj, carry):
        step(kj, False)
        return carry

    lax.fori_loop(0, qi, full_block, 0)
    step(qi, True)

    lv = lam_ref[...]
    lam = (jnp.exp(jnp.sum(lv[0:1] * lv[1:2], axis=-1, keepdims=True))
           - jnp.exp(jnp.sum(lv[2:3] * lv[3:4], axis=-1, keepdims=True)) + lam_init)
    ot = acc_ref[0] * (1.0 / l_ref[0]) - lam * (acc_ref[1] * (1.0 / l_ref[1]))
    ms = jnp.mean(ot * ot, axis=0, keepdims=True)
    ot = ot * lax.rsqrt(ms + LN_EPS)
    o_ref[...] = (ot.T * g_ref[...] * (1.0 - lam_init)).astype(o_ref.dtype)


def _diff_attention(qt, k, vt, lam_params, subln_g, layer, lam_init, *, batch, seq, tq):
    T, W = k.shape
    hw = 2 * HEAD_DIM
    heads = W // hw
    tq = _tile(seq, tq)
    nq = seq // tq
    blocks = 2 * _nbytes((tq, hw), BF16) + 2 * _nbytes((seq, hw), BF16)
    scratch = 2 * _nbytes((tq, hw), F32) + 4 * _nbytes((SUBLANES, tq), F32)
    return pl.pallas_call(
        functools.partial(_attn_kernel, tq=tq, lam_init=lam_init),
        name=f"diff_attn_l{layer}",
        grid=(batch, heads, nq),
        in_specs=[
            pl.BlockSpec((None, 4, HEAD_DIM), lambda b, h, i: (layer, 0, 0)),
            pl.BlockSpec((None, 1, hw), lambda b, h, i: (layer, 0, 0)),
            pl.BlockSpec((hw, tq), lambda b, h, i: (h, b * nq + i)),
            pl.BlockSpec((seq, hw), lambda b, h, i: (b, h)),
            pl.BlockSpec((hw, seq), lambda b, h, i: (h, b)),
        ],
        out_specs=pl.BlockSpec((tq, hw), lambda b, h, i: (b * nq + i, h)),
        out_shape=jax.ShapeDtypeStruct((T, W), BF16),
        scratch_shapes=[
            pltpu.VMEM((2, 1, tq), F32),
            pltpu.VMEM((2, 1, tq), F32),
            pltpu.VMEM((2, hw, tq), F32),
        ],
        compiler_params=pltpu.CompilerParams(
            dimension_semantics=("parallel", "parallel", "arbitrary"),
            vmem_limit_bytes=_vmem_limit(blocks, scratch, 6 * _nbytes((tq, tq), F32))),
    )(lam_params, subln_g, qt, k, vt)


def _rnn_kernel(xr_ref, gr_ref, cw_ref, cb_ref, wrg_ref, brg_ref, lam_ref, o_ref,
                xbuf, a_s, b_s, h_s, hcar, *, tt, nblk):
    t = pl.program_id(1)
    C = xr_ref.shape[1]
    bw = C // nblk

    @pl.when(t == 0)
    def _():
        xbuf[0:SUBLANES, :] = jnp.zeros((SUBLANES, C), F32)
        hcar[...] = jnp.zeros(hcar.shape, F32)

    x = xr_ref[...]
    xbuf[SUBLANES:SUBLANES + tt, :] = x
    cw = cw_ref[...]
    xc = cw[CONV_WIDTH - 1:CONV_WIDTH] * x + cb_ref[...]
    for d in range(1, CONV_WIDTH):
        xc = xc + cw[CONV_WIDTH - 1 - d:CONV_WIDTH - d] * xbuf[SUBLANES - d:SUBLANES - d + tt, :]
    xbuf[0:SUBLANES, :] = x[tt - SUBLANES:tt, :]

    sub = lax.broadcasted_iota(jnp.int32, (tt // SUBLANES, SUBLANES, bw), 1)
    for n in range(nblk):
        cs = slice(n * bw, (n + 1) * bw)
        xcn = xc[:, cs]
        xb = xcn.astype(BF16)
        gr_pre = jnp.dot(xb, wrg_ref[0, n], preferred_element_type=F32) + brg_ref[0:1, cs]
        gi_pre = jnp.dot(xb, wrg_ref[1, n], preferred_element_type=F32) + brg_ref[1:2, cs]
        r = jax.nn.sigmoid(gr_pre)
        i = jax.nn.sigmoid(gi_pre)
        log_a = (-LRU_C) * r * jax.nn.softplus(-lam_ref[:, cs])
        a = jnp.exp(log_a)
        u = jnp.sqrt((1.0 - a) * (1.0 + a)) * (i * xcn)
        a3 = a.reshape(tt // SUBLANES, SUBLANES, bw)
        u3 = u.reshape(tt // SUBLANES, SUBLANES, bw)
        for s in (1, 2, 4):
            keep = sub >= s
            u_new = a3 * pltpu.roll(u3, s, axis=1) + u3
            a_new = a3 * pltpu.roll(a3, s, axis=1)
            u3 = jnp.where(keep, u_new, u3)
            a3 = jnp.where(keep, a_new, a3)
        a_s[:, cs] = a3.reshape(tt, bw)
        b_s[:, cs] = u3.reshape(tt, bw)

    def group(j, h):
        rows = pl.ds(pl.multiple_of(j * SUBLANES, SUBLANES), SUBLANES)
        hj = a_s[rows, :] * h + b_s[rows, :]
        h_s[rows, :] = hj
        return jnp.broadcast_to(hj[SUBLANES - 1:SUBLANES, :], hj.shape)

    hcar[...] = lax.fori_loop(0, tt // SUBLANES, group, hcar[...])
    o_ref[...] = (h_s[...] * jax.nn.gelu(gr_ref[...])).astype(o_ref.dtype)


def _rglru(rest, conv_w, conv_b, w_rg, b_rg, lru_lambda, layer, *, batch, seq, width, tt):
    T = rest.shape[0]
    nblk = w_rg.shape[2]
    tt = _tile(seq, tt)
    nt = seq // tt
    blocks = 2 * _nbytes((tt, width), F32) + _nbytes((tt, width), BF16) + _nbytes(w_rg.shape[1:], BF16)
    scratch = 4 * _nbytes((tt + SUBLANES, width), F32)
    return pl.pallas_call(
        functools.partial(_rnn_kernel, tt=tt, nblk=nblk),
        name=f"rglru_l{layer}",
        grid=(batch, nt),
        in_specs=[
            pl.BlockSpec((tt, width), lambda b, t: (b * nt + t, 0)),
            pl.BlockSpec((tt, width), lambda b, t: (b * nt + t, 1)),
            pl.BlockSpec((None, CONV_WIDTH, width), lambda b, t: (layer, 0, 0)),
            pl.BlockSpec((None, 1, width), lambda b, t: (layer, 0, 0)),
            pl.BlockSpec((None,) + w_rg.shape[1:], lambda b, t: (layer, 0, 0, 0, 0)),
            pl.BlockSpec((None, 2, width), lambda b, t: (layer, 0, 0)),
            pl.BlockSpec((None, 1, width), lambda b, t: (layer, 0, 0)),
        ],
        out_specs=pl.BlockSpec((tt, width), lambda b, t: (b * nt + t, 0)),
        out_shape=jax.ShapeDtypeStruct((T, width), BF16),
        scratch_shapes=[
            pltpu.VMEM((tt + SUBLANES, width), F32),
            pltpu.VMEM((tt, width), F32),
            pltpu.VMEM((tt, width), F32),
            pltpu.VMEM((tt, width), F32),
            pltpu.VMEM((SUBLANES, width), F32),
        ],
        compiler_params=pltpu.CompilerParams(
            dimension_semantics=("parallel", "arbitrary"),
            vmem_limit_bytes=_vmem_limit(blocks, scratch, 6 * _nbytes((tt, width), F32))),
    )(rest, rest, conv_w, conv_b, w_rg, b_rg, lru_lambda)


def _merge_kernel(oa_ref, hr_ref, wa_ref, wr_ref, ga_ref, gr_ref, bm_ref, o_ref):
    ya = jnp.dot(oa_ref[...], wa_ref[...], preferred_element_type=F32)
    yr = jnp.dot(hr_ref[...], wr_ref[...], preferred_element_type=F32)
    ga = jax.nn.sigmoid(ga_ref[...] + bm_ref[0:1, :])
    gr = jax.nn.sigmoid(gr_ref[...] + bm_ref[1:2, :])
    o_ref[...] = (ga * ya + gr * yr).astype(o_ref.dtype)


def _merge(o_attn, h_rnn, w_branch, rest, b_merge, layer, *, tm, tn):
    T, K = o_attn.shape
    D = w_branch.shape[-1]
    tm = _tile(T, tm)
    tn = _tile(D, tn)
    nb = D // tn
    gate0 = (rest.shape[1] - 2 * D) // tn
    blocks = (2 * _nbytes((tm, K), BF16) + 2 * _nbytes((K, tn), BF16) + 2 * _nbytes((tm, tn), F32)
              + _nbytes((tm, tn), BF16))
    return pl.pallas_call(
        _merge_kernel,
        name=f"merge_l{layer}",
        grid=(T // tm, nb),
        in_specs=[
            pl.BlockSpec((tm, K), lambda i, j: (i, 0)),
            pl.BlockSpec((tm, K), lambda i, j: (i, 0)),
            pl.BlockSpec((None, None, K, tn), lambda i, j: (layer, 0, 0, j)),
            pl.BlockSpec((None, None, K, tn), lambda i, j: (layer, 1, 0, j)),
            pl.BlockSpec((tm, tn), lambda i, j: (i, gate0 + j)),
            pl.BlockSpec((tm, tn), lambda i, j: (i, gate0 + nb + j)),
            pl.BlockSpec((None, 2, tn), lambda i, j: (layer, 0, j)),
        ],
        out_specs=pl.BlockSpec((tm, tn), lambda i, j: (i, j)),
        out_shape=jax.ShapeDtypeStruct((T, D), BF16),
        compiler_params=pltpu.CompilerParams(
            dimension_semantics=("parallel", "arbitrary"),
            vmem_limit_bytes=_vmem_limit(blocks, temp_bytes=4 * _nbytes((tm, tn), F32))),
    )(o_attn, h_rnn, w_branch, w_branch, rest, rest, b_merge)


def _layer_norm_rows(xf, g, b):
    mu = jnp.mean(xf, axis=-1, keepdims=True)
    d = xf - mu
    var = jnp.mean(d * d, axis=-1, keepdims=True)
    return d * lax.rsqrt(var + LN_EPS) * g + b


def _outproj_ln_kernel(m_ref, w_ref, x_ref, g_ref, b_ref, of_ref, ob_ref, *, alpha, which):
    mix = jnp.dot(m_ref[...], w_ref[...], preferred_element_type=F32)
    y = _layer_norm_rows(alpha * x_ref[...] + mix, g_ref[which:which + 1, :], b_ref[which:which + 1, :])
    of_ref[...] = y
    ob_ref[...] = y.astype(BF16)


def _outproj_ln(merged, w_out, x, ln_g, ln_b, layer, alpha, *, tm):
    T, K = merged.shape
    D = w_out.shape[-1]
    tm = _tile(T, tm)
    blocks = (_nbytes((tm, K), BF16) + _nbytes((K, D), BF16) + 2 * _nbytes((tm, D), F32)
              + _nbytes((tm, D), BF16))
    return pl.pallas_call(
        functools.partial(_outproj_ln_kernel, alpha=alpha, which=0),
        name=f"outproj_ln_l{layer}",
        grid=(T // tm,),
        in_specs=[
            pl.BlockSpec((tm, K), lambda i: (i, 0)),
            pl.BlockSpec((None, K, D), lambda i: (layer, 0, 0)),
            pl.BlockSpec((tm, D), lambda i: (i, 0)),
            pl.BlockSpec((None, 2, D), lambda i: (layer, 0, 0)),
            pl.BlockSpec((None, 2, D), lambda i: (layer, 0, 0)),
        ],
        out_specs=[pl.BlockSpec((tm, D), lambda i: (i, 0)), pl.BlockSpec((tm, D), lambda i: (i, 0))],
        out_shape=[jax.ShapeDtypeStruct((T, D), F32), jax.ShapeDtypeStruct((T, D), BF16)],
        compiler_params=pltpu.CompilerParams(
            dimension_semantics=("parallel",),
            vmem_limit_bytes=_vmem_limit(blocks, temp_bytes=3 * _nbytes((tm, D), F32))),
    )(merged, w_out, x, ln_g, ln_b)


def _ffn_ln_kernel(xb_ref, wg_ref, wu_ref, wd_ref, x_ref, g_ref, b_ref, of_ref, ob_ref, acc_ref, *, alpha):
    f = pl.program_id(1)
    xb = xb_ref[...]
    hg = jnp.dot(xb, wg_ref[...], preferred_element_type=F32)
    hu = jnp.dot(xb, wu_ref[...], preferred_element_type=F32)
    act = (jax.nn.silu(hg) * hu).astype(BF16)
    part = jnp.dot(act, wd_ref[...], preferred_element_type=F32)

    @pl.when(f == 0)
    def _():
        acc_ref[...] = part

    @pl.when(f > 0)
    def _():
        acc_ref[...] += part

    @pl.when(f == pl.num_programs(1) - 1)
    def _():
        y = _layer_norm_rows(alpha * x_ref[...] + acc_ref[...], g_ref[1:2, :], b_ref[1:2, :])
        of_ref[...] = y
        ob_ref[...] = y.astype(BF16)


def _ffn_ln(xb, x, w_gate_up, w_down, ln_g, ln_b, layer, alpha, *, tm, tf):
    T, D = x.shape
    F = w_down.shape[1]
    tm = _tile(T, tm)
    tf = _tile(F, tf)
    nf = F // tf
    blocks = (_nbytes((tm, D), BF16) + 3 * _nbytes((D, tf), BF16) + 2 * _nbytes((tm, D), F32)
              + _nbytes((tm, D), BF16))
    return pl.pallas_call(
        functools.partial(_ffn_ln_kernel, alpha=alpha),
        name=f"ffn_ln_l{layer}",
        grid=(T // tm, nf),
        in_specs=[
            pl.BlockSpec((tm, D), lambda i, f: (i, 0)),
            pl.BlockSpec((None, D, tf), lambda i, f: (layer, 0, f)),
            pl.BlockSpec((None, D, tf), lambda i, f: (layer, 0, nf + f)),
            pl.BlockSpec((None, tf, D), lambda i, f: (layer, f, 0)),
            pl.BlockSpec((tm, D), lambda i, f: (i, 0)),
            pl.BlockSpec((None, 2, D), lambda i, f: (layer, 0, 0)),
            pl.BlockSpec((None, 2, D), lambda i, f: (layer, 0, 0)),
        ],
        out_specs=[pl.BlockSpec((tm, D), lambda i, f: (i, 0)), pl.BlockSpec((tm, D), lambda i, f: (i, 0))],
        out_shape=[jax.ShapeDtypeStruct((T, D), F32), jax.ShapeDtypeStruct((T, D), BF16)],
        scratch_shapes=[pltpu.VMEM((tm, D), F32)],
        compiler_params=pltpu.CompilerParams(
            dimension_semantics=("parallel", "arbitrary"),
            vmem_limit_bytes=_vmem_limit(blocks, _nbytes((tm, D), F32),
                                         3 * _nbytes((tm, tf), F32) + _nbytes((tm, D), F32))),
    )(xb, w_gate_up, w_gate_up, w_down, x, ln_g, ln_b)


def _rope_tables(seq):
    half = HEAD_DIM // 2
    inv_freq = ROPE_THETA ** (-jnp.arange(half, dtype=F32) * 2.0 / HEAD_DIM)
    ang = jnp.arange(seq, dtype=F32)[:, None] * inv_freq[None, :]
    ang = jnp.concatenate([ang, ang], axis=-1)
    sign = jnp.concatenate([-jnp.ones((half,), F32), jnp.ones((half,), F32)])
    return jnp.cos(ang), jnp.sin(ang) * sign


def kernel(x, w_in, b_merge, diff_lambda, subln_g, conv_w, conv_b, w_rg, b_rg, lru_lambda,
           w_branch, w_out, ln_g, ln_b, w_gate_up, w_down):
    B, S, D = x.shape
    T = B * S
    depth = w_in.shape[0]
    attn_w = w_branch.shape[2]
    d_rnn = conv_w.shape[-1]
    qk_w = (w_in.shape[-1] - attn_w - 2 * d_rnn - 2 * D) // 2
    alpha = (2.0 * depth) ** 0.25

    w_in_b = w_in.astype(BF16)
    wq_t = jnp.swapaxes(w_in[:, :, :qk_w], 1, 2).astype(BF16)
    wv_t = jnp.swapaxes(w_in[:, :, 2 * qk_w:2 * qk_w + attn_w], 1, 2).astype(BF16)
    w_rg_b = w_rg.astype(BF16)
    w_branch_b = w_branch.astype(BF16)
    w_out_b = w_out.astype(BF16)
    w_gate_up_b = w_gate_up.astype(BF16)
    w_down_b = w_down.astype(BF16)
    subln_g3 = subln_g.reshape(depth, 1, -1)
    conv_b3 = conv_b.reshape(depth, 1, -1)
    lru_lambda3 = lru_lambda.reshape(depth, 1, -1)
    cos_k, sin_k = _rope_tables(S)
    q_scale = HEAD_DIM ** -0.5 * math.log2(math.e)
    rope_q_t = (cos_k.T * q_scale, sin_k.T * q_scale)
    rope_k = (cos_k[None], sin_k[None])

    xf = x.reshape(T, D)
    xb = xf.astype(BF16)
    for l in range(depth):
        lam_init = 0.8 - 0.6 * math.exp(-0.3 * l)
        qt = _project_t(xb, wq_t, l, "proj_qt", tm=1024, tn=1024, rope=rope_q_t, seq=S)
        k = _project(xb, w_in_b, l, qk_w, qk_w, BF16, tm=1024, tn=1024, rope=rope_k, seq=S)
        vt = _project_t(xb, wv_t, l, "proj_vt", tm=1024, tn=1024)
        rest = _project(xb, w_in_b, l, 2 * qk_w + attn_w, 2 * d_rnn + 2 * D, F32, tm=1024, tn=1024)
        o_attn = _diff_attention(qt, k, vt, diff_lambda, subln_g3, l, lam_init,
                                 batch=B, seq=S, tq=512)
        h_rnn = _rglru(rest, conv_w, conv_b3, w_rg_b, b_rg, lru_lambda3, l,
                       batch=B, seq=S, width=d_rnn, tt=512)
        merged = _merge(o_attn, h_rnn, w_branch_b, rest, b_merge, l, tm=1024, tn=512)
        xf, xb = _outproj_ln(merged, w_out_b, xf, ln_g, ln_b, l, alpha, tm=512)
        xf, xb = _ffn_ln(xb, xf, w_gate_up_b, w_down_b, ln_g, ln_b, l, alpha, tm=512, tf=512)
    return xf.reshape(B, S, D)
```

```python
import functools
import math

import jax
import jax.numpy as jnp
from jax import lax
from jax.experimental import pallas as pl
from jax.experimental.pallas import tpu as pltpu

HEAD_DIM = 128
ROPE_THETA = 10000.0
CONV_WIDTH = 4
LRU_C = 8.0
LN_EPS = 1e-5

LANES = 128
SUBLANES = 8
V7X_VMEM_BYTES = 64 * 1024 * 1024
V7X_VMEM_CEILING = V7X_VMEM_BYTES - 6 * 1024 * 1024

BF16 = jnp.bfloat16
F32 = jnp.float32

FFN_CHUNK = 512


def _vmem_limit(pipelined_bytes, resident_bytes=0, temp_bytes=0):
    need = 2 * pipelined_bytes + resident_bytes + temp_bytes + (4 << 20)
    return int(min(max(need, 16 << 20), V7X_VMEM_CEILING))


def _nbytes(shape, dtype):
    return math.prod(shape) * jnp.dtype(dtype).itemsize


def _tile(dim, want):
    t = min(dim, want)
    while dim % t:
        t //= 2
    return t


def _proj_kernel(x_ref, w_ref, o_ref):
    acc = jnp.dot(x_ref[...], w_ref[...], preferred_element_type=F32)
    o_ref[...] = acc.astype(o_ref.dtype)


def _proj_rope_kernel(x_ref, w_ref, cos_ref, sin_ref, o_ref):
    acc = jnp.dot(x_ref[...], w_ref[...], preferred_element_type=F32)
    cos = cos_ref[...]
    sin = sin_ref[...]
    for c in range(acc.shape[1] // HEAD_DIM):
        t = acc[:, c * HEAD_DIM:(c + 1) * HEAD_DIM]
        rot = pltpu.roll(t, HEAD_DIM // 2, axis=1)
        o_ref[:, c * HEAD_DIM:(c + 1) * HEAD_DIM] = (t * cos + rot * sin).astype(o_ref.dtype)


def _project(x, w_stack, layer, col0, ncols, out_dtype, *, tm, tn, rope=None, seq=None):
    T, K = x.shape
    tm = _tile(T, tm)
    tn = _tile(ncols, tn)
    assert col0 % tn == 0
    cb0 = col0 // tn
    grid = (T // tm, ncols // tn)
    in_specs = [
        pl.BlockSpec((tm, K), lambda i, j: (i, 0)),
        pl.BlockSpec((None, K, tn), lambda i, j: (layer, 0, j + cb0)),
    ]
    args = [x, w_stack]
    blocks = _nbytes((tm, K), BF16) + _nbytes((K, tn), BF16) + _nbytes((tm, tn), out_dtype)
    if rope is None:
        body = _proj_kernel
    else:
        cos_t, sin_t = rope
        groups = cos_t.shape[0]
        per = (ncols // tn) // groups
        tms = _tile(seq, tm)
        assert tms == tm
        nsb = seq // tm
        tab_spec = pl.BlockSpec((None, tm, HEAD_DIM), lambda i, j: (j // per, i % nsb, 0))
        in_specs += [tab_spec, tab_spec]
        args += [cos_t, sin_t]
        blocks += 2 * _nbytes((tm, HEAD_DIM), F32)
        body = _proj_rope_kernel
    return pl.pallas_call(
        body,
        name=f"proj_c{col0}_l{layer}",
        grid=grid,
        in_specs=in_specs,
        out_specs=pl.BlockSpec((tm, tn), lambda i, j: (i, j)),
        out_shape=jax.ShapeDtypeStruct((T, ncols), out_dtype),
        compiler_params=pltpu.CompilerParams(
            dimension_semantics=("parallel", "arbitrary"),
            vmem_limit_bytes=_vmem_limit(blocks, temp_bytes=2 * _nbytes((tm, tn), F32))),
    )(*args)


_NT = (((1,), (1,)), ((), ()))


def _proj_t_kernel(w_ref, x_ref, o_ref):
    acc = lax.dot_general(w_ref[...], x_ref[...], _NT, preferred_element_type=F32)
    o_ref[...] = acc.astype(o_ref.dtype)


def _proj_t_rope_kernel(w_ref, x_ref, cos_ref, sin_ref, o_ref):
    acc = lax.dot_general(w_ref[...], x_ref[...], _NT, preferred_element_type=F32)
    cos = cos_ref[...]
    sin = sin_ref[...]
    half = HEAD_DIM // 2
    for c in range(acc.shape[0] // HEAD_DIM):
        t = acc[c * HEAD_DIM:(c + 1) * HEAD_DIM, :]
        rot = jnp.concatenate([t[half:], t[:half]], axis=0)
        o_ref[c * HEAD_DIM:(c + 1) * HEAD_DIM, :] = (t * cos + rot * sin).astype(o_ref.dtype)


def _project_t(x, wt_stack, layer, name, *, tm, tn, rope=None, seq=None):
    T, K = x.shape
    N = wt_stack.shape[1]
    tm = _tile(T, tm)
    tn = _tile(N, tn)
    in_specs = [
        pl.BlockSpec((None, tn, K), lambda i, j: (layer, j, 0)),
        pl.BlockSpec((tm, K), lambda i, j: (i, 0)),
    ]
    args = [wt_stack, x]
    blocks = _nbytes((tm, K), BF16) + _nbytes((K, tn), BF16) + _nbytes((tm, tn), BF16)
    if rope is None:
        body = _proj_t_kernel
    else:
        assert seq % tm == 0
        nsb = seq // tm
        tab_spec = pl.BlockSpec((HEAD_DIM, tm), lambda i, j: (0, i % nsb))
        in_specs += [tab_spec, tab_spec]
        args += list(rope)
        blocks += 2 * _nbytes((tm, HEAD_DIM), F32)
        body = _proj_t_rope_kernel
    return pl.pallas_call(
        body,
        name=f"{name}_l{layer}",
        grid=(T // tm, N // tn),
        in_specs=in_specs,
        out_specs=pl.BlockSpec((tn, tm), lambda i, j: (j, i)),
        out_shape=jax.ShapeDtypeStruct((N, T), BF16),
        compiler_params=pltpu.CompilerParams(
            dimension_semantics=("parallel", "arbitrary"),
            vmem_limit_bytes=_vmem_limit(blocks, temp_bytes=2 * _nbytes((tm, tn), F32))),
    )(*args)


def _attn_kernel(lam_ref, g_ref, qt_ref, k_ref, vt_ref, o_ref, m_ref, l_ref, acc_ref, sa_ref, sb_ref,
                 *, tq, lam_init):
    qi = pl.program_id(2)
    m_ref[...] = jnp.full(m_ref.shape, -jnp.inf, F32)
    l_ref[...] = jnp.zeros(l_ref.shape, F32)
    acc_ref[...] = jnp.zeros(acc_ref.shape, F32)

    def scores_into(s_ref, kj):
        kblk = k_ref[pl.ds(pl.multiple_of(kj * tq, tq), tq), :]
        for c in range(2):
            s_ref[c] = jnp.dot(kblk[:, c * HEAD_DIM:(c + 1) * HEAD_DIM],
                               qt_ref[c * HEAD_DIM:(c + 1) * HEAD_DIM, :], preferred_element_type=F32)

    def softmax_pv(s_ref, kj, masked):
        vtblk = vt_ref[:, pl.ds(pl.multiple_of(kj * tq, tq), tq)]
        for c in range(2):
            st = s_ref[c]
            if masked:
                key = lax.broadcasted_iota(jnp.int32, st.shape, 0)
                qry = lax.broadcasted_iota(jnp.int32, st.shape, 1)
                st = jnp.where(qry >= key, st, -jnp.inf)
            m_old = m_ref[c]
            m_new = jnp.maximum(m_old, jnp.max(st, axis=0, keepdims=True))
            alpha = jnp.exp2(m_old - m_new)
            pt = jnp.exp2(st - m_new)
            l_ref[c] = alpha * l_ref[c] + jnp.sum(pt, axis=0, keepdims=True)
            acc_ref[c] = alpha * acc_ref[c] + jnp.dot(vtblk, pt.astype(BF16), preferred_element_type=F32)
            m_ref[c] = m_new

    def visible_block(this_ref, next_ref, kj):
        scores_into(next_ref, kj + 1)
        softmax_pv(this_ref, kj, False)

    def round_of_two(r, carry):
        visible_block(sa_ref, sb_ref, 2 * r)
        visible_block(sb_ref, sa_ref, 2 * r + 1)
        return carry

    scores_into(sa_ref, 0)
    lax.fori_loop(0, qi // 2, round_of_two, 0)

    @pl.when(qi % 2 == 0)
    def _():
        softmax_pv(sa_ref, qi, True)

    @pl.when(qi % 2 == 1)
    def _():
        visible_block(sa_ref, sb_ref, qi - 1)
        softmax_pv(sb_ref, qi, True)

    lv = lam_ref[...]
    lam = (jnp.exp(jnp.sum(lv[0:1] * lv[1:2], axis=-1, keepdims=True))
           - jnp.exp(jnp.sum(lv[2:3] * lv[3:4], axis=-1, keepdims=True)) + lam_init)
    ot = acc_ref[0] * (1.0 / l_ref[0]) - lam * (acc_ref[1] * (1.0 / l_ref[1]))
    ms = jnp.mean(ot * ot, axis=0, keepdims=True)
    ot = ot * lax.rsqrt(ms + LN_EPS)
    o_ref[...] = (ot.T * g_ref[...] * (1.0 - lam_init)).astype(o_ref.dtype)


def _diff_attention(qt, k, vt, lam_params, subln_g, layer, lam_init, *, batch, seq, tq):
    T, W = k.shape
    hw = 2 * HEAD_DIM
    heads = W // hw
    tq = _tile(seq, tq)
    nq = seq // tq
    blocks = 2 * _nbytes((tq, hw), BF16) + 2 * _nbytes((seq, hw), BF16)
    scratch = 2 * _nbytes((tq, hw), F32) + 4 * _nbytes((SUBLANES, tq), F32) + 4 * _nbytes((tq, tq), F32)
    return pl.pallas_call(
        functools.partial(_attn_kernel, tq=tq, lam_init=lam_init),
        name=f"diff_attn_l{layer}",
        grid=(batch, heads, nq),
        in_specs=[
            pl.BlockSpec((None, 4, HEAD_DIM), lambda b, h, i: (layer, 0, 0)),
            pl.BlockSpec((None, 1, hw), lambda b, h, i: (layer, 0, 0)),
            pl.BlockSpec((hw, tq), lambda b, h, i: (h, b * nq + i)),
            pl.BlockSpec((seq, hw), lambda b, h, i: (b, h)),
            pl.BlockSpec((hw, seq), lambda b, h, i: (h, b)),
        ],
        out_specs=pl.BlockSpec((tq, hw), lambda b, h, i: (b * nq + i, h)),
        out_shape=jax.ShapeDtypeStruct((T, W), BF16),
        scratch_shapes=[
            pltpu.VMEM((2, 1, tq), F32),
            pltpu.VMEM((2, 1, tq), F32),
            pltpu.VMEM((2, hw, tq), F32),
            pltpu.VMEM((2, tq, tq), F32),
            pltpu.VMEM((2, tq, tq), F32),
        ],
        compiler_params=pltpu.CompilerParams(
            dimension_semantics=("parallel", "parallel", "arbitrary"),
            vmem_limit_bytes=_vmem_limit(blocks, scratch, 6 * _nbytes((tq, tq), F32))),
    )(lam_params, subln_g, qt, k, vt)


def _rnn_kernel(xr_ref, gr_ref, cw_ref, cb_ref, wrg_ref, brg_ref, lam_ref, o_ref,
                xbuf, a_s, b_s, h_s, hcar, *, tt, nblk):
    t = pl.program_id(1)
    C = xr_ref.shape[1]
    bw = C // nblk

    @pl.when(t == 0)
    def _():
        xbuf[0:SUBLANES, :] = jnp.zeros((SUBLANES, C), F32)
        hcar[...] = jnp.zeros(hcar.shape, F32)

    x = xr_ref[...]
    xbuf[SUBLANES:SUBLANES + tt, :] = x
    cw = cw_ref[...]
    xc = cw[CONV_WIDTH - 1:CONV_WIDTH] * x + cb_ref[...]
    for d in range(1, CONV_WIDTH):
        xc = xc + cw[CONV_WIDTH - 1 - d:CONV_WIDTH - d] * xbuf[SUBLANES - d:SUBLANES - d + tt, :]
    xbuf[0:SUBLANES, :] = x[tt - SUBLANES:tt, :]

    sub = lax.broadcasted_iota(jnp.int32, (tt // SUBLANES, SUBLANES, bw), 1)
    for n in range(nblk):
        cs = slice(n * bw, (n + 1) * bw)
        xcn = xc[:, cs]
        xb = xcn.astype(BF16)
        gr_pre = jnp.dot(xb, wrg_ref[0, n], preferred_element_type=F32) + brg_ref[0:1, cs]
        gi_pre = jnp.dot(xb, wrg_ref[1, n], preferred_element_type=F32) + brg_ref[1:2, cs]
        r = jax.nn.sigmoid(gr_pre)
        i = jax.nn.sigmoid(gi_pre)
        log_a = (-LRU_C) * r * jax.nn.softplus(-lam_ref[:, cs])
        a = jnp.exp(log_a)
        u = jnp.sqrt((1.0 - a) * (1.0 + a)) * (i * xcn)
        a3 = a.reshape(tt // SUBLANES, SUBLANES, bw)
        u3 = u.reshape(tt // SUBLANES, SUBLANES, bw)
        for s in (1, 2, 4):
            keep = sub >= s
            u_new = a3 * pltpu.roll(u3, s, axis=1) + u3
            a_new = a3 * pltpu.roll(a3, s, axis=1)
            u3 = jnp.where(keep, u_new, u3)
            a3 = jnp.where(keep, a_new, a3)
        a_s[:, cs] = a3.reshape(tt, bw)
        b_s[:, cs] = u3.reshape(tt, bw)

    def group(j, h):
        rows = pl.ds(pl.multiple_of(j * SUBLANES, SUBLANES), SUBLANES)
        hj = a_s[rows, :] * h + b_s[rows, :]
        h_s[rows, :] = hj
        return jnp.broadcast_to(hj[SUBLANES - 1:SUBLANES, :], hj.shape)

    hcar[...] = lax.fori_loop(0, tt // SUBLANES, group, hcar[...])
    o_ref[...] = (h_s[...] * jax.nn.gelu(gr_ref[...])).astype(o_ref.dtype)


def _rglru(rest, conv_w, conv_b, w_rg, b_rg, lru_lambda, layer, *, batch, seq, width, tt):
    T = rest.shape[0]
    nblk = w_rg.shape[2]
    tt = _tile(seq, tt)
    nt = seq // tt
    blocks = 2 * _nbytes((tt, width), F32) + _nbytes((tt, width), BF16) + _nbytes(w_rg.shape[1:], BF16)
    scratch = 4 * _nbytes((tt + SUBLANES, width), F32)
    return pl.pallas_call(
        functools.partial(_rnn_kernel, tt=tt, nblk=nblk),
        name=f"rglru_l{layer}",
        grid=(batch, nt),
        in_specs=[
            pl.BlockSpec((tt, width), lambda b, t: (b * nt + t, 0)),
            pl.BlockSpec((tt, width), lambda b, t: (b * nt + t, 1)),
            pl.BlockSpec((None, CONV_WIDTH, width), lambda b, t: (layer, 0, 0)),
            pl.BlockSpec((None, 1, width), lambda b, t: (layer, 0, 0)),
            pl.BlockSpec((None,) + w_rg.shape[1:], lambda b, t: (layer, 0, 0, 0, 0)),
            pl.BlockSpec((None, 2, width), lambda b, t: (layer, 0, 0)),
            pl.BlockSpec((None, 1, width), lambda b, t: (layer, 0, 0)),
        ],
        out_specs=pl.BlockSpec((tt, width), lambda b, t: (b * nt + t, 0)),
        out_shape=jax.ShapeDtypeStruct((T, width), BF16),
        scratch_shapes=[
            pltpu.VMEM((tt + SUBLANES, width), F32),
            pltpu.VMEM((tt, width), F32),
            pltpu.VMEM((tt, width), F32),
            pltpu.VMEM((tt, width), F32),
            pltpu.VMEM((SUBLANES, width), F32),
        ],
        compiler_params=pltpu.CompilerParams(
            dimension_semantics=("parallel", "arbitrary"),
            vmem_limit_bytes=_vmem_limit(blocks, scratch, 6 * _nbytes((tt, width), F32))),
    )(rest, rest, conv_w, conv_b, w_rg, b_rg, lru_lambda)


def _merge_kernel(oa_ref, hr_ref, wa_ref, wr_ref, ga_ref, gr_ref, bm_ref, o_ref):
    ya = jnp.dot(oa_ref[...], wa_ref[...], preferred_element_type=F32)
    yr = jnp.dot(hr_ref[...], wr_ref[...], preferred_element_type=F32)
    ga = jax.nn.sigmoid(ga_ref[...] + bm_ref[0:1, :])
    gr = jax.nn.sigmoid(gr_ref[...] + bm_ref[1:2, :])
    o_ref[...] = (ga * ya + gr * yr).astype(o_ref.dtype)


def _merge(o_attn, h_rnn, w_branch, rest, b_merge, layer, *, tm, tn):
    T, K = o_attn.shape
    D = w_branch.shape[-1]
    tm = _tile(T, tm)
    tn = _tile(D, tn)
    nb = D // tn
    gate0 = (rest.shape[1] - 2 * D) // tn
    blocks = (2 * _nbytes((tm, K), BF16) + 2 * _nbytes((K, tn), BF16) + 2 * _nbytes((tm, tn), F32)
              + _nbytes((tm, tn), BF16))
    return pl.pallas_call(
        _merge_kernel,
        name=f"merge_l{layer}",
        grid=(T // tm, nb),
        in_specs=[
            pl.BlockSpec((tm, K), lambda i, j: (i, 0)),
            pl.BlockSpec((tm, K), lambda i, j: (i, 0)),
            pl.BlockSpec((None, None, K, tn), lambda i, j: (layer, 0, 0, j)),
            pl.BlockSpec((None, None, K, tn), lambda i, j: (layer, 1, 0, j)),
            pl.BlockSpec((tm, tn), lambda i, j: (i, gate0 + j)),
            pl.BlockSpec((tm, tn), lambda i, j: (i, gate0 + nb + j)),
            pl.BlockSpec((None, 2, tn), lambda i, j: (layer, 0, j)),
        ],
        out_specs=pl.BlockSpec((tm, tn), lambda i, j: (i, j)),
        out_shape=jax.ShapeDtypeStruct((T, D), BF16),
        compiler_params=pltpu.CompilerParams(
            dimension_semantics=("parallel", "arbitrary"),
            vmem_limit_bytes=_vmem_limit(blocks, temp_bytes=4 * _nbytes((tm, tn), F32))),
    )(o_attn, h_rnn, w_branch, w_branch, rest, rest, b_merge)


def _layer_norm_rows(xf, g, b):
    mu = jnp.mean(xf, axis=-1, keepdims=True)
    d = xf - mu
    var = jnp.mean(d * d, axis=-1, keepdims=True)
    return d * lax.rsqrt(var + LN_EPS) * g + b


def _outproj_ln_kernel(m_ref, w_ref, x_ref, g_ref, b_ref, of_ref, ob_ref, *, alpha, which):
    mix = jnp.dot(m_ref[...], w_ref[...], preferred_element_type=F32)
    y = _layer_norm_rows(alpha * x_ref[...] + mix, g_ref[which:which + 1, :], b_ref[which:which + 1, :])
    of_ref[...] = y
    ob_ref[...] = y.astype(BF16)


def _outproj_ln(merged, w_out, x, ln_g, ln_b, layer, alpha, *, tm):
    T, K = merged.shape
    D = w_out.shape[-1]
    tm = _tile(T, tm)
    blocks = (_nbytes((tm, K), BF16) + _nbytes((K, D), BF16) + 2 * _nbytes((tm, D), F32)
              + _nbytes((tm, D), BF16))
    return pl.pallas_call(
        functools.partial(_outproj_ln_kernel, alpha=alpha, which=0),
        name=f"outproj_ln_l{layer}",
        grid=(T // tm,),
        in_specs=[
            pl.BlockSpec((tm, K), lambda i: (i, 0)),
            pl.BlockSpec((None, K, D), lambda i: (layer, 0, 0)),
            pl.BlockSpec((tm, D), lambda i: (i, 0)),
            pl.BlockSpec((None, 2, D), lambda i: (layer, 0, 0)),
            pl.BlockSpec((None, 2, D), lambda i: (layer, 0, 0)),
        ],
        out_specs=[pl.BlockSpec((tm, D), lambda i: (i, 0)), pl.BlockSpec((tm, D), lambda i: (i, 0))],
        out_shape=[jax.ShapeDtypeStruct((T, D), F32), jax.ShapeDtypeStruct((T, D), BF16)],
        compiler_params=pltpu.CompilerParams(
            dimension_semantics=("parallel",),
            vmem_limit_bytes=_vmem_limit(blocks, temp_bytes=3 * _nbytes((tm, D), F32))),
    )(merged, w_out, x, ln_g, ln_b)


def _ffn_ln_kernel(xb_ref, wgu_ref, wd_ref, x_ref, g_ref, b_ref, of_ref, ob_ref, acc_ref, *, alpha):
    f = pl.program_id(1)
    tf = wd_ref.shape[0]

    @pl.when(f == 0)
    def _():
        acc_ref[...] = jnp.zeros(acc_ref.shape, F32)

    h = jnp.dot(xb_ref[...], wgu_ref[...], preferred_element_type=F32)
    act = (jax.nn.silu(h[:, :tf]) * h[:, tf:]).astype(BF16)
    acc_ref[...] += jnp.dot(act, wd_ref[...], preferred_element_type=F32)

    @pl.when(f == pl.num_programs(1) - 1)
    def _():
        y = _layer_norm_rows(alpha * x_ref[...] + acc_ref[...], g_ref[1:2, :], b_ref[1:2, :])
        of_ref[...] = y
        ob_ref[...] = y.astype(BF16)


def _ffn_ln(xb, x, w_gate_up, w_down, ln_g, ln_b, layer, alpha, *, tm, tf):
    T, D = x.shape
    F = w_down.shape[1]
    tm = _tile(T, tm)
    assert F % tf == 0
    nf = F // tf
    blocks = (_nbytes((tm, D), BF16) + 3 * _nbytes((D, tf), BF16) + 2 * _nbytes((tm, D), F32)
              + _nbytes((tm, D), BF16))
    return pl.pallas_call(
        functools.partial(_ffn_ln_kernel, alpha=alpha),
        name=f"ffn_ln_l{layer}",
        grid=(T // tm, nf),
        in_specs=[
            pl.BlockSpec((tm, D), lambda i, f: (i, 0)),
            pl.BlockSpec((None, D, 2 * tf), lambda i, f: (layer, 0, f)),
            pl.BlockSpec((None, tf, D), lambda i, f: (layer, f, 0)),
            pl.BlockSpec((tm, D), lambda i, f: (i, 0)),
            pl.BlockSpec((None, 2, D), lambda i, f: (layer, 0, 0)),
            pl.BlockSpec((None, 2, D), lambda i, f: (layer, 0, 0)),
        ],
        out_specs=[pl.BlockSpec((tm, D), lambda i, f: (i, 0)), pl.BlockSpec((tm, D), lambda i, f: (i, 0))],
        out_shape=[jax.ShapeDtypeStruct((T, D), F32), jax.ShapeDtypeStruct((T, D), BF16)],
        scratch_shapes=[pltpu.VMEM((tm, D), F32)],
        compiler_params=pltpu.CompilerParams(
            dimension_semantics=("parallel", "arbitrary"),
            vmem_limit_bytes=_vmem_limit(blocks, _nbytes((tm, D), F32),
                                         3 * _nbytes((tm, tf), F32) + _nbytes((tm, D), F32))),
    )(xb, w_gate_up, w_down, x, ln_g, ln_b)


def _rope_tables(seq):
    half = HEAD_DIM // 2
    inv_freq = ROPE_THETA ** (-jnp.arange(half, dtype=F32) * 2.0 / HEAD_DIM)
    ang = jnp.arange(seq, dtype=F32)[:, None] * inv_freq[None, :]
    ang = jnp.concatenate([ang, ang], axis=-1)
    sign = jnp.concatenate([-jnp.ones((half,), F32), jnp.ones((half,), F32)])
    return jnp.cos(ang), jnp.sin(ang) * sign


def kernel(x, w_in, b_merge, diff_lambda, subln_g, conv_w, conv_b, w_rg, b_rg, lru_lambda,
           w_branch, w_out, ln_g, ln_b, w_gate_up, w_down):
    B, S, D = x.shape
    T = B * S
    depth = w_in.shape[0]
    attn_w = w_branch.shape[2]
    d_rnn = conv_w.shape[-1]
    qk_w = (w_in.shape[-1] - attn_w - 2 * d_rnn - 2 * D) // 2
    alpha = (2.0 * depth) ** 0.25

    w_in_b = w_in.astype(BF16)
    wq_t = jnp.swapaxes(w_in[:, :, :qk_w], 1, 2).astype(BF16)
    wv_t = jnp.swapaxes(w_in[:, :, 2 * qk_w:2 * qk_w + attn_w], 1, 2).astype(BF16)
    w_rg_b = w_rg.astype(BF16)
    w_branch_b = w_branch.astype(BF16)
    w_out_b = w_out.astype(BF16)
    d_ff = w_down.shape[1]
    w_gate_up_b = (w_gate_up.reshape(depth, D, 2, d_ff // FFN_CHUNK, FFN_CHUNK).swapaxes(2, 3)
                   .reshape(depth, D, 2 * d_ff).astype(BF16))
    w_down_b = w_down.astype(BF16)
    subln_g3 = subln_g.reshape(depth, 1, -1)
    conv_b3 = conv_b.reshape(depth, 1, -1)
    lru_lambda3 = lru_lambda.reshape(depth, 1, -1)
    cos_k, sin_k = _rope_tables(S)
    q_scale = HEAD_DIM ** -0.5 * math.log2(math.e)
    rope_q_t = (cos_k.T * q_scale, sin_k.T * q_scale)
    rope_k = (cos_k[None], sin_k[None])

    xf = x.reshape(T, D)
    xb = xf.astype(BF16)
    for l in range(depth):
        lam_init = 0.8 - 0.6 * math.exp(-0.3 * l)
        qt = _project_t(xb, wq_t, l, "proj_qt", tm=1024, tn=1024, rope=rope_q_t, seq=S)
        k = _project(xb, w_in_b, l, qk_w, qk_w, BF16, tm=1024, tn=1024, rope=rope_k, seq=S)
        vt = _project_t(xb, wv_t, l, "proj_vt", tm=1024, tn=1024)
        rest = _project(xb, w_in_b, l, 2 * qk_w + attn_w, 2 * d_rnn + 2 * D, F32, tm=1024, tn=1024)
        o_attn = _diff_attention(qt, k, vt, diff_lambda, subln_g3, l, lam_init,
                                 batch=B, seq=S, tq=512)
        h_rnn = _rglru(rest, conv_w, conv_b3, w_rg_b, b_rg, lru_lambda3, l,
                       batch=B, seq=S, width=d_rnn, tt=512)
        merged = _merge(o_attn, h_rnn, w_branch_b, rest, b_merge, l, tm=1024, tn=512)
        xf, xb = _outproj_ln(merged, w_out_b, xf, ln_g, ln_b, l, alpha, tm=512)
        xf, xb = _ffn_ln(xb, xf, w_gate_up_b, w_down_b, ln_g, ln_b, l, alpha, tm=512, tf=FFN_CHUNK)
    return xf.reshape(B, S, D)
```

```python
import functools
import math

import jax
import jax.numpy as jnp
from jax import lax
from jax.experimental import pallas as pl
from jax.experimental.pallas import tpu as pltpu

HEAD_DIM = 128
ROPE_THETA = 10000.0
CONV_WIDTH = 4
LRU_C = 8.0
LN_EPS = 1e-5

LANES = 128
SUBLANES = 8
V7X_VMEM_BYTES = 64 * 1024 * 1024
V7X_VMEM_CEILING = V7X_VMEM_BYTES - 6 * 1024 * 1024

BF16 = jnp.bfloat16
F32 = jnp.float32

FFN_CHUNK = 512


def _vmem_limit(pipelined_bytes, resident_bytes=0, temp_bytes=0):
    need = 2 * pipelined_bytes + resident_bytes + temp_bytes + (4 << 20)
    return int(min(max(need, 16 << 20), V7X_VMEM_CEILING))


def _nbytes(shape, dtype):
    return math.prod(shape) * jnp.dtype(dtype).itemsize


def _sigmoid(x):
    return 0.5 * jnp.tanh(0.5 * x) + 0.5


def _gelu_tanh(x):
    c = math.sqrt(2.0 / math.pi)
    inner = x * (c + (c * 0.044715) * (x * x))
    return x * (0.5 * jnp.tanh(inner) + 0.5)


def _tile(dim, want):
    t = min(dim, want)
    while dim % t:
        t //= 2
    return t


def _proj_kernel(x_ref, w_ref, o_ref):
    acc = jnp.dot(x_ref[...], w_ref[...], preferred_element_type=F32)
    o_ref[...] = acc.astype(o_ref.dtype)


def _proj_rope_kernel(x_ref, w_ref, cos_ref, sin_ref, o_ref):
    acc = jnp.dot(x_ref[...], w_ref[...], preferred_element_type=F32)
    cos = cos_ref[...]
    sin = sin_ref[...]
    for c in range(acc.shape[1] // HEAD_DIM):
        t = acc[:, c * HEAD_DIM:(c + 1) * HEAD_DIM]
        rot = pltpu.roll(t, HEAD_DIM // 2, axis=1)
        o_ref[:, c * HEAD_DIM:(c + 1) * HEAD_DIM] = (t * cos + rot * sin).astype(o_ref.dtype)


def _project(x, w_stack, layer, col0, ncols, out_dtype, *, tm, tn, rope=None, seq=None):
    T, K = x.shape
    tm = _tile(T, tm)
    tn = _tile(ncols, tn)
    assert col0 % tn == 0
    cb0 = col0 // tn
    grid = (T // tm, ncols // tn)
    in_specs = [
        pl.BlockSpec((tm, K), lambda i, j: (i, 0)),
        pl.BlockSpec((None, K, tn), lambda i, j: (layer, 0, j + cb0)),
    ]
    args = [x, w_stack]
    blocks = _nbytes((tm, K), BF16) + _nbytes((K, tn), BF16) + _nbytes((tm, tn), out_dtype)
    if rope is None:
        body = _proj_kernel
    else:
        cos_t, sin_t = rope
        groups = cos_t.shape[0]
        per = (ncols // tn) // groups
        tms = _tile(seq, tm)
        assert tms == tm
        nsb = seq // tm
        tab_spec = pl.BlockSpec((None, tm, HEAD_DIM), lambda i, j: (j // per, i % nsb, 0))
        in_specs += [tab_spec, tab_spec]
        args += [cos_t, sin_t]
        blocks += 2 * _nbytes((tm, HEAD_DIM), F32)
        body = _proj_rope_kernel
    return pl.pallas_call(
        body,
        name=f"proj_c{col0}_l{layer}",
        grid=grid,
        in_specs=in_specs,
        out_specs=pl.BlockSpec((tm, tn), lambda i, j: (i, j)),
        out_shape=jax.ShapeDtypeStruct((T, ncols), out_dtype),
        compiler_params=pltpu.CompilerParams(
            dimension_semantics=("parallel", "arbitrary"),
            vmem_limit_bytes=_vmem_limit(blocks, temp_bytes=2 * _nbytes((tm, tn), F32))),
    )(*args)


_NT = (((1,), (1,)), ((), ()))


def _proj_t_kernel(w_ref, x_ref, o_ref):
    acc = lax.dot_general(w_ref[...], x_ref[...], _NT, preferred_element_type=F32)
    o_ref[...] = acc.astype(o_ref.dtype)


def _proj_t_rope_kernel(w_ref, x_ref, cos_ref, sin_ref, o_ref):
    acc = lax.dot_general(w_ref[...], x_ref[...], _NT, preferred_element_type=F32)
    cos = cos_ref[...]
    sin = sin_ref[...]
    half = HEAD_DIM // 2
    for c in range(acc.shape[0] // HEAD_DIM):
        t = acc[c * HEAD_DIM:(c + 1) * HEAD_DIM, :]
        rot = jnp.concatenate([t[half:], t[:half]], axis=0)
        o_ref[c * HEAD_DIM:(c + 1) * HEAD_DIM, :] = (t * cos + rot * sin).astype(o_ref.dtype)


def _project_t(x, wt_stack, layer, name, *, tm, tn, rope=None, seq=None):
    T, K = x.shape
    N = wt_stack.shape[1]
    tm = _tile(T, tm)
    tn = _tile(N, tn)
    in_specs = [
        pl.BlockSpec((None, tn, K), lambda i, j: (layer, j, 0)),
        pl.BlockSpec((tm, K), lambda i, j: (i, 0)),
    ]
    args = [wt_stack, x]
    blocks = _nbytes((tm, K), BF16) + _nbytes((K, tn), BF16) + _nbytes((tm, tn), BF16)
    if rope is None:
        body = _proj_t_kernel
    else:
        assert seq % tm == 0
        nsb = seq // tm
        tab_spec = pl.BlockSpec((HEAD_DIM, tm), lambda i, j: (0, i % nsb))
        in_specs += [tab_spec, tab_spec]
        args += list(rope)
        blocks += 2 * _nbytes((tm, HEAD_DIM), F32)
        body = _proj_t_rope_kernel
    return pl.pallas_call(
        body,
        name=f"{name}_l{layer}",
        grid=(T // tm, N // tn),
        in_specs=in_specs,
        out_specs=pl.BlockSpec((tn, tm), lambda i, j: (j, i)),
        out_shape=jax.ShapeDtypeStruct((N, T), BF16),
        compiler_params=pltpu.CompilerParams(
            dimension_semantics=("parallel", "arbitrary"),
            vmem_limit_bytes=_vmem_limit(blocks, temp_bytes=2 * _nbytes((tm, tn), F32))),
    )(*args)


def _attn_kernel(lam_ref, g_ref, qt_ref, k_ref, vt_ref, o_ref, m_ref, l_ref, acc_ref, sa_ref, sb_ref,
                 *, tq, lam_init):
    qi = pl.program_id(2)
    m_ref[...] = jnp.full(m_ref.shape, -jnp.inf, F32)
    l_ref[...] = jnp.zeros(l_ref.shape, F32)
    acc_ref[...] = jnp.zeros(acc_ref.shape, F32)

    def scores_into(s_ref, kj):
        kblk = k_ref[pl.ds(pl.multiple_of(kj * tq, tq), tq), :]
        for c in range(2):
            s_ref[c] = jnp.dot(kblk[:, c * HEAD_DIM:(c + 1) * HEAD_DIM],
                               qt_ref[c * HEAD_DIM:(c + 1) * HEAD_DIM, :], preferred_element_type=F32)

    def softmax_pv(s_ref, kj, masked):
        vtblk = vt_ref[:, pl.ds(pl.multiple_of(kj * tq, tq), tq)]
        for c in range(2):
            st = s_ref[c]
            if masked:
                key = lax.broadcasted_iota(jnp.int32, st.shape, 0)
                qry = lax.broadcasted_iota(jnp.int32, st.shape, 1)
                st = jnp.where(qry >= key, st, -jnp.inf)
            m_old = m_ref[c]
            m_new = jnp.maximum(m_old, jnp.max(st, axis=0, keepdims=True))
            alpha = jnp.exp2(m_old - m_new)
            pt = jnp.exp2(st - m_new)
            l_ref[c] = alpha * l_ref[c] + jnp.sum(pt, axis=0, keepdims=True)
            acc_ref[c] = alpha * acc_ref[c] + jnp.dot(vtblk, pt.astype(BF16), preferred_element_type=F32)
            m_ref[c] = m_new

    def visible_block(this_ref, next_ref, kj):
        scores_into(next_ref, kj + 1)
        softmax_pv(this_ref, kj, False)

    def round_of_two(r, carry):
        visible_block(sa_ref, sb_ref, 2 * r)
        visible_block(sb_ref, sa_ref, 2 * r + 1)
        return carry

    scores_into(sa_ref, 0)
    lax.fori_loop(0, qi // 2, round_of_two, 0)

    @pl.when(qi % 2 == 0)
    def _():
        softmax_pv(sa_ref, qi, True)

    @pl.when(qi % 2 == 1)
    def _():
        visible_block(sa_ref, sb_ref, qi - 1)
        softmax_pv(sb_ref, qi, True)

    lv = lam_ref[...]
    lam = (jnp.exp(jnp.sum(lv[0:1] * lv[1:2], axis=-1, keepdims=True))
           - jnp.exp(jnp.sum(lv[2:3] * lv[3:4], axis=-1, keepdims=True)) + lam_init)
    ot = acc_ref[0] * (1.0 / l_ref[0]) - lam * (acc_ref[1] * (1.0 / l_ref[1]))
    ms = jnp.mean(ot * ot, axis=0, keepdims=True)
    ot = ot * lax.rsqrt(ms + LN_EPS)
    o_ref[...] = (ot.T * g_ref[...] * (1.0 - lam_init)).astype(o_ref.dtype)


def _diff_attention(qt, k, vt, lam_params, subln_g, layer, lam_init, *, batch, seq, tq):
    T, W = k.shape
    hw = 2 * HEAD_DIM
    heads = W // hw
    tq = _tile(seq, tq)
    nq = seq // tq
    blocks = 2 * _nbytes((tq, hw), BF16) + 2 * _nbytes((seq, hw), BF16)
    scratch = 2 * _nbytes((tq, hw), F32) + 4 * _nbytes((SUBLANES, tq), F32) + 4 * _nbytes((tq, tq), F32)
    return pl.pallas_call(
        functools.partial(_attn_kernel, tq=tq, lam_init=lam_init),
        name=f"diff_attn_l{layer}",
        grid=(batch, heads, nq),
        in_specs=[
            pl.BlockSpec((None, 4, HEAD_DIM), lambda b, h, i: (layer, 0, 0)),
            pl.BlockSpec((None, 1, hw), lambda b, h, i: (layer, 0, 0)),
            pl.BlockSpec((hw, tq), lambda b, h, i: (h, b * nq + i)),
            pl.BlockSpec((seq, hw), lambda b, h, i: (b, h)),
            pl.BlockSpec((hw, seq), lambda b, h, i: (h, b)),
        ],
        out_specs=pl.BlockSpec((tq, hw), lambda b, h, i: (b * nq + i, h)),
        out_shape=jax.ShapeDtypeStruct((T, W), BF16),
        scratch_shapes=[
            pltpu.VMEM((2, 1, tq), F32),
            pltpu.VMEM((2, 1, tq), F32),
            pltpu.VMEM((2, hw, tq), F32),
            pltpu.VMEM((2, tq, tq), F32),
            pltpu.VMEM((2, tq, tq), F32),
        ],
        compiler_params=pltpu.CompilerParams(
            dimension_semantics=("parallel", "parallel", "arbitrary"),
            vmem_limit_bytes=_vmem_limit(blocks, scratch, 6 * _nbytes((tq, tq), F32))),
    )(lam_params, subln_g, qt, k, vt)


def _rnn_kernel(xr_ref, gr_ref, cw_ref, cb_ref, wrg_ref, brg_ref, lam_ref, o_ref,
                xprev, a_s, b_s, h_s, hcar, *, tt, nblk):
    t = pl.program_id(1)
    C = xr_ref.shape[1]
    bw = C // nblk

    @pl.when(t == 0)
    def _():
        xprev[...] = jnp.zeros(xprev.shape, F32)
        hcar[...] = jnp.zeros(hcar.shape, F32)

    ng = tt // SUBLANES
    x3 = xr_ref[...].reshape(ng, SUBLANES, C)
    tail = xprev[...]
    xprev[...] = x3[ng - 1]
    cw = cw_ref[...]
    sub_c = lax.broadcasted_iota(jnp.int32, (ng, SUBLANES, C), 1)
    xc3 = cw[CONV_WIDTH - 1:CONV_WIDTH] * x3 + cb_ref[...]
    for d in range(1, CONV_WIDTH):
        rolled = pltpu.roll(x3, d, axis=1)
        rolled_prev = jnp.concatenate([pltpu.roll(tail, d, axis=0)[None], rolled[:ng - 1]], axis=0)
        xc3 = xc3 + cw[CONV_WIDTH - 1 - d:CONV_WIDTH - d] * jnp.where(sub_c >= d, rolled, rolled_prev)
    xc = xc3.reshape(tt, C)

    sub = lax.broadcasted_iota(jnp.int32, (tt // SUBLANES, SUBLANES, bw), 1)
    for n in range(nblk):
        cs = slice(n * bw, (n + 1) * bw)
        xcn = xc[:, cs]
        xb = xcn.astype(BF16)
        gr_pre = jnp.dot(xb, wrg_ref[0, n], preferred_element_type=F32) + brg_ref[0:1, cs]
        gi_pre = jnp.dot(xb, wrg_ref[1, n], preferred_element_type=F32) + brg_ref[1:2, cs]
        i = _sigmoid(gi_pre)
        half_rate = (-0.5 * LRU_C) * jax.nn.softplus(-lam_ref[:, cs])
        log_a = half_rate * jnp.tanh(0.5 * gr_pre) + half_rate
        a = jnp.exp(log_a)
        t = (1.0 - a) * (1.0 + a)
        u = jnp.where(t > 0.0, t * lax.rsqrt(t), 0.0) * (i * xcn)
        a3 = a.reshape(tt // SUBLANES, SUBLANES, bw)
        u3 = u.reshape(tt // SUBLANES, SUBLANES, bw)
        for s in (1, 2, 4):
            keep = sub >= s
            u_new = a3 * pltpu.roll(u3, s, axis=1) + u3
            a_new = a3 * pltpu.roll(a3, s, axis=1)
            u3 = jnp.where(keep, u_new, u3)
            a3 = jnp.where(keep, a_new, a3)
        a_s[:, cs] = a3.reshape(tt, bw)
        b_s[:, cs] = u3.reshape(tt, bw)

    def group(j, h):
        rows = pl.ds(pl.multiple_of(j * SUBLANES, SUBLANES), SUBLANES)
        hj = a_s[rows, :] * h + b_s[rows, :]
        h_s[rows, :] = hj
        return jnp.broadcast_to(hj[SUBLANES - 1:SUBLANES, :], hj.shape)

    hcar[...] = lax.fori_loop(0, tt // SUBLANES, group, hcar[...])
    o_ref[...] = (h_s[...] * _gelu_tanh(gr_ref[...])).astype(o_ref.dtype)


def _rglru(rest, conv_w, conv_b, w_rg, b_rg, lru_lambda, layer, *, batch, seq, width, tt):
    T = rest.shape[0]
    nblk = w_rg.shape[2]
    tt = _tile(seq, tt)
    nt = seq // tt
    blocks = 2 * _nbytes((tt, width), F32) + _nbytes((tt, width), BF16) + _nbytes(w_rg.shape[1:], BF16)
    scratch = 4 * _nbytes((tt + SUBLANES, width), F32)
    return pl.pallas_call(
        functools.partial(_rnn_kernel, tt=tt, nblk=nblk),
        name=f"rglru_l{layer}",
        grid=(batch, nt),
        in_specs=[
            pl.BlockSpec((tt, width), lambda b, t: (b * nt + t, 0)),
            pl.BlockSpec((tt, width), lambda b, t: (b * nt + t, 1)),
            pl.BlockSpec((None, CONV_WIDTH, width), lambda b, t: (layer, 0, 0)),
            pl.BlockSpec((None, 1, width), lambda b, t: (layer, 0, 0)),
            pl.BlockSpec((None,) + w_rg.shape[1:], lambda b, t: (layer, 0, 0, 0, 0)),
            pl.BlockSpec((None, 2, width), lambda b, t: (layer, 0, 0)),
            pl.BlockSpec((None, 1, width), lambda b, t: (layer, 0, 0)),
        ],
        out_specs=pl.BlockSpec((tt, width), lambda b, t: (b * nt + t, 0)),
        out_shape=jax.ShapeDtypeStruct((T, width), BF16),
        scratch_shapes=[
            pltpu.VMEM((SUBLANES, width), F32),
            pltpu.VMEM((tt, width), F32),
            pltpu.VMEM((tt, width), F32),
            pltpu.VMEM((tt, width), F32),
            pltpu.VMEM((SUBLANES, width), F32),
        ],
        compiler_params=pltpu.CompilerParams(
            dimension_semantics=("parallel", "arbitrary"),
            vmem_limit_bytes=_vmem_limit(blocks, scratch, 6 * _nbytes((tt, width), F32))),
    )(rest, rest, conv_w, conv_b, w_rg, b_rg, lru_lambda)


def _merge_kernel(oa_ref, hr_ref, wa_ref, wr_ref, ga_ref, gr_ref, bm_ref, o_ref):
    ya = jnp.dot(oa_ref[...], wa_ref[...], preferred_element_type=F32)
    yr = jnp.dot(hr_ref[...], wr_ref[...], preferred_element_type=F32)
    ga = _sigmoid(ga_ref[...] + bm_ref[0:1, :])
    gr = _sigmoid(gr_ref[...] + bm_ref[1:2, :])
    o_ref[...] = (ga * ya + gr * yr).astype(o_ref.dtype)


def _merge(o_attn, h_rnn, w_branch, rest, b_merge, layer, *, tm, tn):
    T, K = o_attn.shape
    D = w_branch.shape[-1]
    tm = _tile(T, tm)
    tn = _tile(D, tn)
    nb = D // tn
    gate0 = (rest.shape[1] - 2 * D) // tn
    blocks = (2 * _nbytes((tm, K), BF16) + 2 * _nbytes((K, tn), BF16) + 2 * _nbytes((tm, tn), F32)
              + _nbytes((tm, tn), BF16))
    return pl.pallas_call(
        _merge_kernel,
        name=f"merge_l{layer}",
        grid=(T // tm, nb),
        in_specs=[
            pl.BlockSpec((tm, K), lambda i, j: (i, 0)),
            pl.BlockSpec((tm, K), lambda i, j: (i, 0)),
            pl.BlockSpec((None, None, K, tn), lambda i, j: (layer, 0, 0, j)),
            pl.BlockSpec((None, None, K, tn), lambda i, j: (layer, 1, 0, j)),
            pl.BlockSpec((tm, tn), lambda i, j: (i, gate0 + j)),
            pl.BlockSpec((tm, tn), lambda i, j: (i, gate0 + nb + j)),
            pl.BlockSpec((None, 2, tn), lambda i, j: (layer, 0, j)),
        ],
        out_specs=pl.BlockSpec((tm, tn), lambda i, j: (i, j)),
        out_shape=jax.ShapeDtypeStruct((T, D), BF16),
        compiler_params=pltpu.CompilerParams(
            dimension_semantics=("parallel", "arbitrary"),
            vmem_limit_bytes=_vmem_limit(blocks, temp_bytes=4 * _nbytes((tm, tn), F32))),
    )(o_attn, h_rnn, w_branch, w_branch, rest, rest, b_merge)


def _layer_norm_rows(xf, g, b):
    mu = jnp.mean(xf, axis=-1, keepdims=True)
    d = xf - mu
    var = jnp.mean(d * d, axis=-1, keepdims=True)
    return d * lax.rsqrt(var + LN_EPS) * g + b


def _outproj_ln_kernel(m_ref, w_ref, x_ref, g_ref, b_ref, of_ref, ob_ref, *, alpha, which):
    mix = jnp.dot(m_ref[...], w_ref[...], preferred_element_type=F32)
    y = _layer_norm_rows(alpha * x_ref[...] + mix, g_ref[which:which + 1, :], b_ref[which:which + 1, :])
    of_ref[...] = y
    ob_ref[...] = y.astype(BF16)


def _outproj_ln(merged, w_out, x, ln_g, ln_b, layer, alpha, *, tm):
    T, K = merged.shape
    D = w_out.shape[-1]
    tm = _tile(T, tm)
    blocks = (_nbytes((tm, K), BF16) + _nbytes((K, D), BF16) + 2 * _nbytes((tm, D), F32)
              + _nbytes((tm, D), BF16))
    return pl.pallas_call(
        functools.partial(_outproj_ln_kernel, alpha=alpha, which=0),
        name=f"outproj_ln_l{layer}",
        grid=(T // tm,),
        in_specs=[
            pl.BlockSpec((tm, K), lambda i: (i, 0)),
            pl.BlockSpec((None, K, D), lambda i: (layer, 0, 0)),
            pl.BlockSpec((tm, D), lambda i: (i, 0)),
            pl.BlockSpec((None, 2, D), lambda i: (layer, 0, 0)),
            pl.BlockSpec((None, 2, D), lambda i: (layer, 0, 0)),
        ],
        out_specs=[pl.BlockSpec((tm, D), lambda i: (i, 0)), pl.BlockSpec((tm, D), lambda i: (i, 0))],
        out_shape=[jax.ShapeDtypeStruct((T, D), F32), jax.ShapeDtypeStruct((T, D), BF16)],
        compiler_params=pltpu.CompilerParams(
            dimension_semantics=("parallel",),
            vmem_limit_bytes=_vmem_limit(blocks, temp_bytes=3 * _nbytes((tm, D), F32))),
    )(merged, w_out, x, ln_g, ln_b)


def _ffn_ln_kernel(xb_ref, wg_ref, wu_ref, wd_ref, x_ref, g_ref, b_ref, of_ref, ob_ref, acc_ref, *, alpha):
    f = pl.program_id(1)

    @pl.when(f == 0)
    def _():
        acc_ref[...] = jnp.zeros(acc_ref.shape, F32)

    xb = xb_ref[...]
    hg = jnp.dot(xb, wg_ref[...], preferred_element_type=F32)
    hu = jnp.dot(xb, wu_ref[...], preferred_element_type=F32)
    act = (hg * _sigmoid(hg) * hu).astype(BF16)
    acc_ref[...] += jnp.dot(act, wd_ref[...], preferred_element_type=F32)

    @pl.when(f == pl.num_programs(1) - 1)
    def _():
        y = _layer_norm_rows(alpha * x_ref[...] + acc_ref[...], g_ref[1:2, :], b_ref[1:2, :])
        of_ref[...] = y
        ob_ref[...] = y.astype(BF16)


def _ffn_ln(xb, x, w_gate_up, w_down, ln_g, ln_b, layer, alpha, *, tm, tf):
    T, D = x.shape
    F = w_down.shape[1]
    tm = _tile(T, tm)
    assert F % tf == 0
    nf = F // tf
    blocks = (_nbytes((tm, D), BF16) + 3 * _nbytes((D, tf), BF16) + 2 * _nbytes((tm, D), F32)
              + _nbytes((tm, D), BF16))
    return pl.pallas_call(
        functools.partial(_ffn_ln_kernel, alpha=alpha),
        name=f"ffn_ln_l{layer}",
        grid=(T // tm, nf),
        in_specs=[
            pl.BlockSpec((tm, D), lambda i, f: (i, 0)),
            pl.BlockSpec((None, D, tf), lambda i, f: (layer, 0, f)),
            pl.BlockSpec((None, D, tf), lambda i, f: (layer, 0, nf + f)),
            pl.BlockSpec((None, tf, D), lambda i, f: (layer, f, 0)),
            pl.BlockSpec((tm, D), lambda i, f: (i, 0)),
            pl.BlockSpec((None, 2, D), lambda i, f: (layer, 0, 0)),
            pl.BlockSpec((None, 2, D), lambda i, f: (layer, 0, 0)),
        ],
        out_specs=[pl.BlockSpec((tm, D), lambda i, f: (i, 0)), pl.BlockSpec((tm, D), lambda i, f: (i, 0))],
        out_shape=[jax.ShapeDtypeStruct((T, D), F32), jax.ShapeDtypeStruct((T, D), BF16)],
        scratch_shapes=[pltpu.VMEM((tm, D), F32)],
        compiler_params=pltpu.CompilerParams(
            dimension_semantics=("parallel", "arbitrary"),
            vmem_limit_bytes=_vmem_limit(blocks, _nbytes((tm, D), F32),
                                         3 * _nbytes((tm, tf), F32) + _nbytes((tm, D), F32))),
    )(xb, w_gate_up, w_gate_up, w_down, x, ln_g, ln_b)


def _rope_tables(seq):
    half = HEAD_DIM // 2
    inv_freq = ROPE_THETA ** (-jnp.arange(half, dtype=F32) * 2.0 / HEAD_DIM)
    ang = jnp.arange(seq, dtype=F32)[:, None] * inv_freq[None, :]
    ang = jnp.concatenate([ang, ang], axis=-1)
    sign = jnp.concatenate([-jnp.ones((half,), F32), jnp.ones((half,), F32)])
    return jnp.cos(ang), jnp.sin(ang) * sign


def kernel(x, w_in, b_merge, diff_lambda, subln_g, conv_w, conv_b, w_rg, b_rg, lru_lambda,
           w_branch, w_out, ln_g, ln_b, w_gate_up, w_down):
    B, S, D = x.shape
    T = B * S
    depth = w_in.shape[0]
    attn_w = w_branch.shape[2]
    d_rnn = conv_w.shape[-1]
    qk_w = (w_in.shape[-1] - attn_w - 2 * d_rnn - 2 * D) // 2
    alpha = (2.0 * depth) ** 0.25

    w_in_b = w_in.astype(BF16)
    wq_t = jnp.swapaxes(w_in[:, :, :qk_w], 1, 2).astype(BF16)
    wv_t = jnp.swapaxes(w_in[:, :, 2 * qk_w:2 * qk_w + attn_w], 1, 2).astype(BF16)
    w_rg_b = w_rg.astype(BF16)
    w_branch_b = w_branch.astype(BF16)
    w_out_b = w_out.astype(BF16)
    w_gate_up_b = w_gate_up.astype(BF16)
    w_down_b = w_down.astype(BF16)
    subln_g3 = subln_g.reshape(depth, 1, -1)
    conv_b3 = conv_b.reshape(depth, 1, -1)
    lru_lambda3 = lru_lambda.reshape(depth, 1, -1)
    cos_k, sin_k = _rope_tables(S)
    q_scale = HEAD_DIM ** -0.5 * math.log2(math.e)
    rope_q_t = (cos_k.T * q_scale, sin_k.T * q_scale)
    rope_k = (cos_k[None], sin_k[None])

    xf = x.reshape(T, D)
    xb = xf.astype(BF16)
    for l in range(depth):
        lam_init = 0.8 - 0.6 * math.exp(-0.3 * l)
        qt = _project_t(xb, wq_t, l, "proj_qt", tm=1024, tn=1024, rope=rope_q_t, seq=S)
        k = _project(xb, w_in_b, l, qk_w, qk_w, BF16, tm=1024, tn=1024, rope=rope_k, seq=S)
        vt = _project_t(xb, wv_t, l, "proj_vt", tm=1024, tn=1024)
        rest = _project(xb, w_in_b, l, 2 * qk_w + attn_w, 2 * d_rnn + 2 * D, F32, tm=1024, tn=1024)
        o_attn = _diff_attention(qt, k, vt, diff_lambda, subln_g3, l, lam_init,
                                 batch=B, seq=S, tq=512)
        h_rnn = _rglru(rest, conv_w, conv_b3, w_rg_b, b_rg, lru_lambda3, l,
                       batch=B, seq=S, width=d_rnn, tt=512)
        merged = _merge(o_attn, h_rnn, w_branch_b, rest, b_merge, l, tm=1024, tn=512)
        xf, xb = _outproj_ln(merged, w_out_b, xf, ln_g, ln_b, l, alpha, tm=512)
        xf, xb = _ffn_ln(xb, xf, w_gate_up_b, w_down_b, ln_g, ln_b, l, alpha, tm=512, tf=FFN_CHUNK)
    return xf.reshape(B, S, D)
```

```python
import functools
import math

import jax
import jax.numpy as jnp
from jax import lax
from jax.experimental import pallas as pl
from jax.experimental.pallas import tpu as pltpu

HEAD_DIM = 128
ROPE_THETA = 10000.0
CONV_WIDTH = 4
LRU_C = 8.0
LN_EPS = 1e-5

LANES = 128
SUBLANES = 8
V7X_VMEM_BYTES = 64 * 1024 * 1024
V7X_VMEM_CEILING = V7X_VMEM_BYTES - 6 * 1024 * 1024

BF16 = jnp.bfloat16
F32 = jnp.float32

FFN_CHUNK = 512


def _vmem_limit(pipelined_bytes, resident_bytes=0, temp_bytes=0):
    need = 2 * pipelined_bytes + resident_bytes + temp_bytes + (4 << 20)
    return int(min(max(need, 16 << 20), V7X_VMEM_CEILING))


def _nbytes(shape, dtype):
    return math.prod(shape) * jnp.dtype(dtype).itemsize


def _sigmoid(x):
    return 0.5 * jnp.tanh(0.5 * x) + 0.5


def _gelu_tanh(x):
    c = math.sqrt(2.0 / math.pi)
    inner = x * (c + (c * 0.044715) * (x * x))
    return x * (0.5 * jnp.tanh(inner) + 0.5)


def _tile(dim, want):
    t = min(dim, want)
    while dim % t:
        t //= 2
    return t


def _proj_kernel(x_ref, w_ref, o_ref):
    acc = jnp.dot(x_ref[...], w_ref[...], preferred_element_type=F32)
    o_ref[...] = acc.astype(o_ref.dtype)


def _proj_rope_kernel(x_ref, w_ref, cos_ref, sin_ref, o_ref):
    acc = jnp.dot(x_ref[...], w_ref[...], preferred_element_type=F32)
    cos = cos_ref[...]
    sin = sin_ref[...]
    for c in range(acc.shape[1] // HEAD_DIM):
        t = acc[:, c * HEAD_DIM:(c + 1) * HEAD_DIM]
        rot = pltpu.roll(t, HEAD_DIM // 2, axis=1)
        o_ref[:, c * HEAD_DIM:(c + 1) * HEAD_DIM] = (t * cos + rot * sin).astype(o_ref.dtype)


def _project(x, w_stack, layer, col0, ncols, out_dtype, *, tm, tn, rope=None, seq=None):
    T, K = x.shape
    tm = _tile(T, tm)
    tn = _tile(ncols, tn)
    assert col0 % tn == 0
    cb0 = col0 // tn
    grid = (T // tm, ncols // tn)
    in_specs = [
        pl.BlockSpec((tm, K), lambda i, j: (i, 0)),
        pl.BlockSpec((None, K, tn), lambda i, j: (layer, 0, j + cb0)),
    ]
    args = [x, w_stack]
    blocks = _nbytes((tm, K), BF16) + _nbytes((K, tn), BF16) + _nbytes((tm, tn), out_dtype)
    if rope is None:
        body = _proj_kernel
    else:
        cos_t, sin_t = rope
        groups = cos_t.shape[0]
        per = (ncols // tn) // groups
        tms = _tile(seq, tm)
        assert tms == tm
        nsb = seq // tm
        tab_spec = pl.BlockSpec((None, tm, HEAD_DIM), lambda i, j: (j // per, i % nsb, 0))
        in_specs += [tab_spec, tab_spec]
        args += [cos_t, sin_t]
        blocks += 2 * _nbytes((tm, HEAD_DIM), F32)
        body = _proj_rope_kernel
    return pl.pallas_call(
        body,
        name=f"proj_c{col0}_l{layer}",
        grid=grid,
        in_specs=in_specs,
        out_specs=pl.BlockSpec((tm, tn), lambda i, j: (i, j)),
        out_shape=jax.ShapeDtypeStruct((T, ncols), out_dtype),
        compiler_params=pltpu.CompilerParams(
            dimension_semantics=("parallel", "arbitrary"),
            vmem_limit_bytes=_vmem_limit(blocks, temp_bytes=2 * _nbytes((tm, tn), F32))),
    )(*args)


_NT = (((1,), (1,)), ((), ()))


def _proj_t_kernel(w_ref, x_ref, o_ref):
    acc = lax.dot_general(w_ref[...], x_ref[...], _NT, preferred_element_type=F32)
    o_ref[...] = acc.astype(o_ref.dtype)


def _proj_t_rope_kernel(w_ref, x_ref, cos_ref, sin_ref, o_ref):
    acc = lax.dot_general(w_ref[...], x_ref[...], _NT, preferred_element_type=F32)
    cos = cos_ref[...]
    sin = sin_ref[...]
    half = HEAD_DIM // 2
    for c in range(acc.shape[0] // HEAD_DIM):
        t = acc[c * HEAD_DIM:(c + 1) * HEAD_DIM, :]
        rot = jnp.concatenate([t[half:], t[:half]], axis=0)
        o_ref[c * HEAD_DIM:(c + 1) * HEAD_DIM, :] = (t * cos + rot * sin).astype(o_ref.dtype)


def _project_t(x, wt_stack, layer, name, *, tm, tn, rope=None, seq=None):
    T, K = x.shape
    N = wt_stack.shape[1]
    tm = _tile(T, tm)
    tn = _tile(N, tn)
    in_specs = [
        pl.BlockSpec((None, tn, K), lambda i, j: (layer, j, 0)),
        pl.BlockSpec((tm, K), lambda i, j: (i, 0)),
    ]
    args = [wt_stack, x]
    blocks = _nbytes((tm, K), BF16) + _nbytes((K, tn), BF16) + _nbytes((tm, tn), BF16)
    if rope is None:
        body = _proj_t_kernel
    else:
        assert seq % tm == 0
        nsb = seq // tm
        tab_spec = pl.BlockSpec((HEAD_DIM, tm), lambda i, j: (0, i % nsb))
        in_specs += [tab_spec, tab_spec]
        args += list(rope)
        blocks += 2 * _nbytes((tm, HEAD_DIM), F32)
        body = _proj_t_rope_kernel
    return pl.pallas_call(
        body,
        name=f"{name}_l{layer}",
        grid=(T // tm, N // tn),
        in_specs=in_specs,
        out_specs=pl.BlockSpec((tn, tm), lambda i, j: (j, i)),
        out_shape=jax.ShapeDtypeStruct((N, T), BF16),
        compiler_params=pltpu.CompilerParams(
            dimension_semantics=("parallel", "arbitrary"),
            vmem_limit_bytes=_vmem_limit(blocks, temp_bytes=2 * _nbytes((tm, tn), F32))),
    )(*args)


def _attn_kernel(lam_ref, g_ref, qt_ref, k_ref, vt_ref, o_ref, m_ref, l_ref, acc_ref,
                 sa_ref, sb_ref, xa_ref, xb_ref, *, tq, nq, lam_init):
    lv = lam_ref[...]
    lam = (jnp.exp(jnp.sum(lv[0:1] * lv[1:2], axis=-1, keepdims=True))
           - jnp.exp(jnp.sum(lv[2:3] * lv[3:4], axis=-1, keepdims=True)) + lam_init)
    gain = g_ref[...] * (1.0 - lam_init)

    def scores_into(s_ref, x_ref, qi, kj, diagonal):
        kblk = k_ref[pl.ds(pl.multiple_of(kj * tq, tq), tq), :]
        qcols = pl.ds(pl.multiple_of(qi * tq, tq), tq)
        for c in range(2):
            st = jnp.dot(kblk[:, c * HEAD_DIM:(c + 1) * HEAD_DIM],
                         qt_ref[c * HEAD_DIM:(c + 1) * HEAD_DIM, qcols], preferred_element_type=F32)
            if diagonal:
                key = lax.broadcasted_iota(jnp.int32, st.shape, 0)
                qry = lax.broadcasted_iota(jnp.int32, st.shape, 1)
                st = jnp.where(qry >= key, st, -jnp.inf)
            s_ref[c] = st
            x_ref[c] = jnp.max(st, axis=0, keepdims=True)

    def softmax_pv(s_ref, x_ref, kj):
        vtblk = vt_ref[:, pl.ds(pl.multiple_of(kj * tq, tq), tq)]
        for c in range(2):
            m_old = m_ref[c]
            m_new = jnp.maximum(m_old, x_ref[c])
            alpha = jnp.exp2(m_old - m_new)
            pt = jnp.exp2(s_ref[c] - m_new)
            l_ref[c] = alpha * l_ref[c] + jnp.sum(pt, axis=0, keepdims=True)
            acc_ref[c] = alpha * acc_ref[c] + jnp.dot(vtblk, pt.astype(BF16), preferred_element_type=F32)
            m_ref[c] = m_new

    A = (sa_ref, xa_ref)
    B = (sb_ref, xb_ref)

    def query_block(qi, carry):
        m_ref[...] = jnp.full(m_ref.shape, -jnp.inf, F32)
        l_ref[...] = jnp.zeros(l_ref.shape, F32)
        acc_ref[...] = jnp.zeros(acc_ref.shape, F32)

        def round_of_two(r, c2):
            scores_into(*B, qi, 2 * r + 1, False)
            softmax_pv(*A, 2 * r)
            scores_into(*A, qi, 2 * r + 2, False)
            softmax_pv(*B, 2 * r + 1)
            return c2

        lax.fori_loop(0, jnp.maximum(qi - 1, 0) // 2, round_of_two, 0)

        @pl.when(qi == 0)
        def _():
            softmax_pv(*A, 0)

        @pl.when(qi % 2 == 1)
        def _():
            scores_into(*B, qi, qi, True)
            softmax_pv(*A, qi - 1)
            softmax_pv(*B, qi)

        @pl.when(jnp.logical_and(qi % 2 == 0, qi > 0))
        def _():
            scores_into(*B, qi, qi - 1, False)
            softmax_pv(*A, qi - 2)
            scores_into(*A, qi, qi, True)
            softmax_pv(*B, qi - 1)
            softmax_pv(*A, qi)

        scores_into(*A, jnp.minimum(qi + 1, nq - 1), 0, False)
        ot = acc_ref[0] * (1.0 / l_ref[0]) - lam * (acc_ref[1] * (1.0 / l_ref[1]))
        ms = jnp.mean(ot * ot, axis=0, keepdims=True)
        ot = ot * lax.rsqrt(ms + LN_EPS)
        o_ref[pl.ds(pl.multiple_of(qi * tq, tq), tq), :] = (ot.T * gain).astype(o_ref.dtype)
        return carry

    scores_into(*A, 0, 0, True)
    lax.fori_loop(0, nq, query_block, 0)


def _diff_attention(qt, k, vt, lam_params, subln_g, layer, lam_init, *, batch, seq, tq):
    T, W = k.shape
    hw = 2 * HEAD_DIM
    heads = W // hw
    tq = _tile(seq, tq)
    nq = seq // tq
    blocks = 4 * _nbytes((seq, hw), BF16)
    scratch = 2 * _nbytes((tq, hw), F32) + 8 * _nbytes((SUBLANES, tq), F32) + 4 * _nbytes((tq, tq), F32)
    return pl.pallas_call(
        functools.partial(_attn_kernel, tq=tq, nq=nq, lam_init=lam_init),
        name=f"diff_attn_l{layer}",
        grid=(batch, heads),
        in_specs=[
            pl.BlockSpec((None, 4, HEAD_DIM), lambda b, h: (layer, 0, 0)),
            pl.BlockSpec((None, 1, hw), lambda b, h: (layer, 0, 0)),
            pl.BlockSpec((hw, seq), lambda b, h: (h, b)),
            pl.BlockSpec((seq, hw), lambda b, h: (b, h)),
            pl.BlockSpec((hw, seq), lambda b, h: (h, b)),
        ],
        out_specs=pl.BlockSpec((seq, hw), lambda b, h: (b, h)),
        out_shape=jax.ShapeDtypeStruct((T, W), BF16),
        scratch_shapes=[
            pltpu.VMEM((2, 1, tq), F32),
            pltpu.VMEM((2, 1, tq), F32),
            pltpu.VMEM((2, hw, tq), F32),
            pltpu.VMEM((2, tq, tq), F32),
            pltpu.VMEM((2, tq, tq), F32),
            pltpu.VMEM((2, 1, tq), F32),
            pltpu.VMEM((2, 1, tq), F32),
        ],
        compiler_params=pltpu.CompilerParams(
            dimension_semantics=("parallel", "parallel"),
            vmem_limit_bytes=_vmem_limit(blocks, scratch, 6 * _nbytes((tq, tq), F32))),
    )(lam_params, subln_g, qt, k, vt)


def _rnn_kernel(xr_ref, gr_ref, cw_ref, cb_ref, wrg_ref, brg_ref, lam_ref, o_ref,
                xprev, a_s, b_s, h_s, hcar, *, tt, nblk):
    t = pl.program_id(1)
    C = xr_ref.shape[1]
    bw = C // nblk

    @pl.when(t == 0)
    def _():
        xprev[...] = jnp.zeros(xprev.shape, F32)
        hcar[...] = jnp.zeros(hcar.shape, F32)

    ng = tt // SUBLANES
    x3 = xr_ref[...].reshape(ng, SUBLANES, C)
    tail = xprev[...]
    xprev[...] = x3[ng - 1]
    cw = cw_ref[...]
    sub_c = lax.broadcasted_iota(jnp.int32, (ng, SUBLANES, C), 1)
    xc3 = cw[CONV_WIDTH - 1:CONV_WIDTH] * x3 + cb_ref[...]
    for d in range(1, CONV_WIDTH):
        rolled = pltpu.roll(x3, d, axis=1)
        rolled_prev = jnp.concatenate([pltpu.roll(tail, d, axis=0)[None], rolled[:ng - 1]], axis=0)
        xc3 = xc3 + cw[CONV_WIDTH - 1 - d:CONV_WIDTH - d] * jnp.where(sub_c >= d, rolled, rolled_prev)
    xc = xc3.reshape(tt, C)

    sub = lax.broadcasted_iota(jnp.int32, (tt // SUBLANES, SUBLANES, bw), 1)
    for n in range(nblk):
        cs = slice(n * bw, (n + 1) * bw)
        xcn = xc[:, cs]
        xb = xcn.astype(BF16)
        gr_pre = jnp.dot(xb, wrg_ref[0, n], preferred_element_type=F32) + brg_ref[0:1, cs]
        gi_pre = jnp.dot(xb, wrg_ref[1, n], preferred_element_type=F32) + brg_ref[1:2, cs]
        i = _sigmoid(gi_pre)
        half_rate = (-0.5 * LRU_C) * jax.nn.softplus(-lam_ref[:, cs])
        log_a = half_rate * jnp.tanh(0.5 * gr_pre) + half_rate
        a = jnp.exp(log_a)
        t = (1.0 - a) * (1.0 + a)
        u = jnp.where(t > 0.0, t * lax.rsqrt(t), 0.0) * (i * xcn)
        a3 = a.reshape(tt // SUBLANES, SUBLANES, bw)
        u3 = u.reshape(tt // SUBLANES, SUBLANES, bw)
        for s in (1, 2, 4):
            keep = sub >= s
            u_new = a3 * pltpu.roll(u3, s, axis=1) + u3
            a_new = a3 * pltpu.roll(a3, s, axis=1)
            u3 = jnp.where(keep, u_new, u3)
            a3 = jnp.where(keep, a_new, a3)
        a_s[:, cs] = a3.reshape(tt, bw)
        b_s[:, cs] = u3.reshape(tt, bw)

    def group(j, h):
        rows = pl.ds(pl.multiple_of(j * SUBLANES, SUBLANES), SUBLANES)
        hj = a_s[rows, :] * h + b_s[rows, :]
        h_s[rows, :] = hj
        return jnp.broadcast_to(hj[SUBLANES - 1:SUBLANES, :], hj.shape)

    hcar[...] = lax.fori_loop(0, tt // SUBLANES, group, hcar[...])
    o_ref[...] = (h_s[...] * _gelu_tanh(gr_ref[...])).astype(o_ref.dtype)


def _rglru(rest, conv_w, conv_b, w_rg, b_rg, lru_lambda, layer, *, batch, seq, width, tt):
    T = rest.shape[0]
    nblk = w_rg.shape[2]
    tt = _tile(seq, tt)
    nt = seq // tt
    blocks = 2 * _nbytes((tt, width), F32) + _nbytes((tt, width), BF16) + _nbytes(w_rg.shape[1:], BF16)
    scratch = 4 * _nbytes((tt + SUBLANES, width), F32)
    return pl.pallas_call(
        functools.partial(_rnn_kernel, tt=tt, nblk=nblk),
        name=f"rglru_l{layer}",
        grid=(batch, nt),
        in_specs=[
            pl.BlockSpec((tt, width), lambda b, t: (b * nt + t, 0)),
            pl.BlockSpec((tt, width), lambda b, t: (b * nt + t, 1)),
            pl.BlockSpec((None, CONV_WIDTH, width), lambda b, t: (layer, 0, 0)),
            pl.BlockSpec((None, 1, width), lambda b, t: (layer, 0, 0)),
            pl.BlockSpec((None,) + w_rg.shape[1:], lambda b, t: (layer, 0, 0, 0, 0)),
            pl.BlockSpec((None, 2, width), lambda b, t: (layer, 0, 0)),
            pl.BlockSpec((None, 1, width), lambda b, t: (layer, 0, 0)),
        ],
        out_specs=pl.BlockSpec((tt, width), lambda b, t: (b * nt + t, 0)),
        out_shape=jax.ShapeDtypeStruct((T, width), BF16),
        scratch_shapes=[
            pltpu.VMEM((SUBLANES, width), F32),
            pltpu.VMEM((tt, width), F32),
            pltpu.VMEM((tt, width), F32),
            pltpu.VMEM((tt, width), F32),
            pltpu.VMEM((SUBLANES, width), F32),
        ],
        compiler_params=pltpu.CompilerParams(
            dimension_semantics=("parallel", "arbitrary"),
            vmem_limit_bytes=_vmem_limit(blocks, scratch, 6 * _nbytes((tt, width), F32))),
    )(rest, rest, conv_w, conv_b, w_rg, b_rg, lru_lambda)


def _merge_kernel(oa_ref, hr_ref, wa_ref, wr_ref, ga_ref, gr_ref, bm_ref, o_ref):
    ya = jnp.dot(oa_ref[...], wa_ref[...], preferred_element_type=F32)
    yr = jnp.dot(hr_ref[...], wr_ref[...], preferred_element_type=F32)
    ga = _sigmoid(ga_ref[...] + bm_ref[0:1, :])
    gr = _sigmoid(gr_ref[...] + bm_ref[1:2, :])
    o_ref[...] = (ga * ya + gr * yr).astype(o_ref.dtype)


def _merge(o_attn, h_rnn, w_branch, rest, b_merge, layer, *, tm, tn):
    T, K = o_attn.shape
    D = w_branch.shape[-1]
    tm = _tile(T, tm)
    tn = _tile(D, tn)
    nb = D // tn
    gate0 = (rest.shape[1] - 2 * D) // tn
    blocks = (2 * _nbytes((tm, K), BF16) + 2 * _nbytes((K, tn), BF16) + 2 * _nbytes((tm, tn), F32)
              + _nbytes((tm, tn), BF16))
    return pl.pallas_call(
        _merge_kernel,
        name=f"merge_l{layer}",
        grid=(T // tm, nb),
        in_specs=[
            pl.BlockSpec((tm, K), lambda i, j: (i, 0)),
            pl.BlockSpec((tm, K), lambda i, j: (i, 0)),
            pl.BlockSpec((None, None, K, tn), lambda i, j: (layer, 0, 0, j)),
            pl.BlockSpec((None, None, K, tn), lambda i, j: (layer, 1, 0, j)),
            pl.BlockSpec((tm, tn), lambda i, j: (i, gate0 + j)),
            pl.BlockSpec((tm, tn), lambda i, j: (i, gate0 + nb + j)),
            pl.BlockSpec((None, 2, tn), lambda i, j: (layer, 0, j)),
        ],
        out_specs=pl.BlockSpec((tm, tn), lambda i, j: (i, j)),
        out_shape=jax.ShapeDtypeStruct((T, D), BF16),
        compiler_params=pltpu.CompilerParams(
            dimension_semantics=("parallel", "arbitrary"),
            vmem_limit_bytes=_vmem_limit(blocks, temp_bytes=4 * _nbytes((tm, tn), F32))),
    )(o_attn, h_rnn, w_branch, w_branch, rest, rest, b_merge)


def _layer_norm_rows(xf, g, b):
    mu = jnp.mean(xf, axis=-1, keepdims=True)
    d = xf - mu
    var = jnp.mean(d * d, axis=-1, keepdims=True)
    return d * lax.rsqrt(var + LN_EPS) * g + b


def _outproj_ln_kernel(m_ref, w_ref, x_ref, g_ref, b_ref, of_ref, ob_ref, *, alpha, which):
    mix = jnp.dot(m_ref[...], w_ref[...], preferred_element_type=F32)
    y = _layer_norm_rows(alpha * x_ref[...] + mix, g_ref[which:which + 1, :], b_ref[which:which + 1, :])
    of_ref[...] = y
    ob_ref[...] = y.astype(BF16)


def _outproj_ln(merged, w_out, x, ln_g, ln_b, layer, alpha, *, tm):
    T, K = merged.shape
    D = w_out.shape[-1]
    tm = _tile(T, tm)
    blocks = (_nbytes((tm, K), BF16) + _nbytes((K, D), BF16) + 2 * _nbytes((tm, D), F32)
              + _nbytes((tm, D), BF16))
    return pl.pallas_call(
        functools.partial(_outproj_ln_kernel, alpha=alpha, which=0),
        name=f"outproj_ln_l{layer}",
        grid=(T // tm,),
        in_specs=[
            pl.BlockSpec((tm, K), lambda i: (i, 0)),
            pl.BlockSpec((None, K, D), lambda i: (layer, 0, 0)),
            pl.BlockSpec((tm, D), lambda i: (i, 0)),
            pl.BlockSpec((None, 2, D), lambda i: (layer, 0, 0)),
            pl.BlockSpec((None, 2, D), lambda i: (layer, 0, 0)),
        ],
        out_specs=[pl.BlockSpec((tm, D), lambda i: (i, 0)), pl.BlockSpec((tm, D), lambda i: (i, 0))],
        out_shape=[jax.ShapeDtypeStruct((T, D), F32), jax.ShapeDtypeStruct((T, D), BF16)],
        compiler_params=pltpu.CompilerParams(
            dimension_semantics=("parallel",),
            vmem_limit_bytes=_vmem_limit(blocks, temp_bytes=3 * _nbytes((tm, D), F32))),
    )(merged, w_out, x, ln_g, ln_b)


def _ffn_ln_kernel(xb_ref, wg_ref, wu_ref, wd_ref, x_ref, g_ref, b_ref, of_ref, ob_ref, acc_ref, *, alpha):
    f = pl.program_id(1)

    @pl.when(f == 0)
    def _():
        acc_ref[...] = jnp.zeros(acc_ref.shape, F32)

    xb = xb_ref[...]
    hg = jnp.dot(xb, wg_ref[...], preferred_element_type=F32)
    hu = jnp.dot(xb, wu_ref[...], preferred_element_type=F32)
    act = (hg * _sigmoid(hg) * hu).astype(BF16)
    acc_ref[...] += jnp.dot(act, wd_ref[...], preferred_element_type=F32)

    @pl.when(f == pl.num_programs(1) - 1)
    def _():
        y = _layer_norm_rows(alpha * x_ref[...] + acc_ref[...], g_ref[1:2, :], b_ref[1:2, :])
        of_ref[...] = y
        ob_ref[...] = y.astype(BF16)


def _ffn_ln(xb, x, w_gate_up, w_down, ln_g, ln_b, layer, alpha, *, tm, tf):
    T, D = x.shape
    F = w_down.shape[1]
    tm = _tile(T, tm)
    assert F % tf == 0
    nf = F // tf
    blocks = (_nbytes((tm, D), BF16) + 3 * _nbytes((D, tf), BF16) + 2 * _nbytes((tm, D), F32)
              + _nbytes((tm, D), BF16))
    return pl.pallas_call(
        functools.partial(_ffn_ln_kernel, alpha=alpha),
        name=f"ffn_ln_l{layer}",
        grid=(T // tm, nf),
        in_specs=[
            pl.BlockSpec((tm, D), lambda i, f: (i, 0)),
            pl.BlockSpec((None, D, tf), lambda i, f: (layer, 0, f)),
            pl.BlockSpec((None, D, tf), lambda i, f: (layer, 0, nf + f)),
            pl.BlockSpec((None, tf, D), lambda i, f: (layer, f, 0)),
            pl.BlockSpec((tm, D), lambda i, f: (i, 0)),
            pl.BlockSpec((None, 2, D), lambda i, f: (layer, 0, 0)),
            pl.BlockSpec((None, 2, D), lambda i, f: (layer, 0, 0)),
        ],
        out_specs=[pl.BlockSpec((tm, D), lambda i, f: (i, 0)), pl.BlockSpec((tm, D), lambda i, f: (i, 0))],
        out_shape=[jax.ShapeDtypeStruct((T, D), F32), jax.ShapeDtypeStruct((T, D), BF16)],
        scratch_shapes=[pltpu.VMEM((tm, D), F32)],
        compiler_params=pltpu.CompilerParams(
            dimension_semantics=("parallel", "arbitrary"),
            vmem_limit_bytes=_vmem_limit(blocks, _nbytes((tm, D), F32),
                                         3 * _nbytes((tm, tf), F32) + _nbytes((tm, D), F32))),
    )(xb, w_gate_up, w_gate_up, w_down, x, ln_g, ln_b)


def _rope_tables(seq):
    half = HEAD_DIM // 2
    inv_freq = ROPE_THETA ** (-jnp.arange(half, dtype=F32) * 2.0 / HEAD_DIM)
    ang = jnp.arange(seq, dtype=F32)[:, None] * inv_freq[None, :]
    ang = jnp.concatenate([ang, ang], axis=-1)
    sign = jnp.concatenate([-jnp.ones((half,), F32), jnp.ones((half,), F32)])
    return jnp.cos(ang), jnp.sin(ang) * sign


def kernel(x, w_in, b_merge, diff_lambda, subln_g, conv_w, conv_b, w_rg, b_rg, lru_lambda,
           w_branch, w_out, ln_g, ln_b, w_gate_up, w_down):
    B, S, D = x.shape
    T = B * S
    depth = w_in.shape[0]
    attn_w = w_branch.shape[2]
    d_rnn = conv_w.shape[-1]
    qk_w = (w_in.shape[-1] - attn_w - 2 * d_rnn - 2 * D) // 2
    alpha = (2.0 * depth) ** 0.25

    w_in_b = w_in.astype(BF16)
    wq_t = jnp.swapaxes(w_in[:, :, :qk_w], 1, 2).astype(BF16)
    wv_t = jnp.swapaxes(w_in[:, :, 2 * qk_w:2 * qk_w + attn_w], 1, 2).astype(BF16)
    w_rg_b = w_rg.astype(BF16)
    w_branch_b = w_branch.astype(BF16)
    w_out_b = w_out.astype(BF16)
    w_gate_up_b = w_gate_up.astype(BF16)
    w_down_b = w_down.astype(BF16)
    subln_g3 = subln_g.reshape(depth, 1, -1)
    conv_b3 = conv_b.reshape(depth, 1, -1)
    lru_lambda3 = lru_lambda.reshape(depth, 1, -1)
    cos_k, sin_k = _rope_tables(S)
    q_scale = HEAD_DIM ** -0.5 * math.log2(math.e)
    rope_q_t = (cos_k.T * q_scale, sin_k.T * q_scale)
    rope_k = (cos_k[None], sin_k[None])

    xf = x.reshape(T, D)
    xb = xf.astype(BF16)
    for l in range(depth):
        lam_init = 0.8 - 0.6 * math.exp(-0.3 * l)
        qt = _project_t(xb, wq_t, l, "proj_qt", tm=1024, tn=1024, rope=rope_q_t, seq=S)
        k = _project(xb, w_in_b, l, qk_w, qk_w, BF16, tm=1024, tn=1024, rope=rope_k, seq=S)
        vt = _project_t(xb, wv_t, l, "proj_vt", tm=1024, tn=1024)
        rest = _project(xb, w_in_b, l, 2 * qk_w + attn_w, 2 * d_rnn + 2 * D, F32, tm=1024, tn=1024)
        o_attn = _diff_attention(qt, k, vt, diff_lambda, subln_g3, l, lam_init,
                                 batch=B, seq=S, tq=512)
        h_rnn = _rglru(rest, conv_w, conv_b3, w_rg_b, b_rg, lru_lambda3, l,
                       batch=B, seq=S, width=d_rnn, tt=512)
        merged = _merge(o_attn, h_rnn, w_branch_b, rest, b_merge, l, tm=1024, tn=512)
        xf, xb = _outproj_ln(merged, w_out_b, xf, ln_g, ln_b, l, alpha, tm=512)
        xf, xb = _ffn_ln(xb, xf, w_gate_up_b, w_down_b, ln_g, ln_b, l, alpha, tm=512, tf=FFN_CHUNK)
    return xf.reshape(B, S, D)
```

```python
import functools
import math

import jax
import jax.numpy as jnp
from jax import lax
from jax.experimental import pallas as pl
from jax.experimental.pallas import tpu as pltpu

HEAD_DIM = 128
ROPE_THETA = 10000.0
CONV_WIDTH = 4
LRU_C = 8.0
LN_EPS = 1e-5

LANES = 128
SUBLANES = 8
V7X_VMEM_BYTES = 64 * 1024 * 1024
V7X_VMEM_CEILING = V7X_VMEM_BYTES - 6 * 1024 * 1024

BF16 = jnp.bfloat16
F32 = jnp.float32

FFN_CHUNK = 512


def _vmem_limit(pipelined_bytes, resident_bytes=0, temp_bytes=0):
    need = 2 * pipelined_bytes + resident_bytes + temp_bytes + (4 << 20)
    return int(min(max(need, 16 << 20), V7X_VMEM_CEILING))


def _nbytes(shape, dtype):
    return math.prod(shape) * jnp.dtype(dtype).itemsize


def _sigmoid(x):
    return 0.5 * jnp.tanh(0.5 * x) + 0.5


def _gelu_tanh(x):
    c = math.sqrt(2.0 / math.pi)
    inner = x * (c + (c * 0.044715) * (x * x))
    return x * (0.5 * jnp.tanh(inner) + 0.5)


def _tile(dim, want):
    t = min(dim, want)
    while dim % t:
        t //= 2
    return t


def _proj_kernel(x_ref, w_ref, o_ref):
    acc = jnp.dot(x_ref[...], w_ref[...], preferred_element_type=F32)
    o_ref[...] = acc.astype(o_ref.dtype)


def _proj_rope_kernel(x_ref, w_ref, cos_ref, sin_ref, o_ref):
    acc = jnp.dot(x_ref[...], w_ref[...], preferred_element_type=F32)
    cos = cos_ref[...]
    sin = sin_ref[...]
    for c in range(acc.shape[1] // HEAD_DIM):
        t = acc[:, c * HEAD_DIM:(c + 1) * HEAD_DIM]
        rot = pltpu.roll(t, HEAD_DIM // 2, axis=1)
        o_ref[:, c * HEAD_DIM:(c + 1) * HEAD_DIM] = (t * cos + rot * sin).astype(o_ref.dtype)


def _project(x, w_stack, layer, col0, ncols, out_dtype, *, tm, tn, rope=None, seq=None):
    T, K = x.shape
    tm = _tile(T, tm)
    tn = _tile(ncols, tn)
    assert col0 % tn == 0
    cb0 = col0 // tn
    grid = (T // tm, ncols // tn)
    in_specs = [
        pl.BlockSpec((tm, K), lambda i, j: (i, 0)),
        pl.BlockSpec((None, K, tn), lambda i, j: (layer, 0, j + cb0)),
    ]
    args = [x, w_stack]
    blocks = _nbytes((tm, K), BF16) + _nbytes((K, tn), BF16) + _nbytes((tm, tn), out_dtype)
    if rope is None:
        body = _proj_kernel
    else:
        cos_t, sin_t = rope
        groups = cos_t.shape[0]
        per = (ncols // tn) // groups
        tms = _tile(seq, tm)
        assert tms == tm
        nsb = seq // tm
        tab_spec = pl.BlockSpec((None, tm, HEAD_DIM), lambda i, j: (j // per, i % nsb, 0))
        in_specs += [tab_spec, tab_spec]
        args += [cos_t, sin_t]
        blocks += 2 * _nbytes((tm, HEAD_DIM), F32)
        body = _proj_rope_kernel
    return pl.pallas_call(
        body,
        name=f"proj_c{col0}_l{layer}",
        grid=grid,
        in_specs=in_specs,
        out_specs=pl.BlockSpec((tm, tn), lambda i, j: (i, j)),
        out_shape=jax.ShapeDtypeStruct((T, ncols), out_dtype),
        compiler_params=pltpu.CompilerParams(
            dimension_semantics=("parallel", "arbitrary"),
            vmem_limit_bytes=_vmem_limit(blocks, temp_bytes=2 * _nbytes((tm, tn), F32))),
    )(*args)


_NT = (((1,), (1,)), ((), ()))


def _proj_t_kernel(w_ref, x_ref, o_ref):
    acc = lax.dot_general(w_ref[...], x_ref[...], _NT, preferred_element_type=F32)
    o_ref[...] = acc.astype(o_ref.dtype)


def _proj_t_rope_kernel(w_ref, x_ref, cos_ref, sin_ref, o_ref):
    acc = lax.dot_general(w_ref[...], x_ref[...], _NT, preferred_element_type=F32)
    cos = cos_ref[...]
    sin = sin_ref[...]
    half = HEAD_DIM // 2
    for c in range(acc.shape[0] // HEAD_DIM):
        t = acc[c * HEAD_DIM:(c + 1) * HEAD_DIM, :]
        rot = jnp.concatenate([t[half:], t[:half]], axis=0)
        o_ref[c * HEAD_DIM:(c + 1) * HEAD_DIM, :] = (t * cos + rot * sin).astype(o_ref.dtype)


def _project_t(x, wt_stack, layer, name, *, tm, tn, rope=None, seq=None):
    T, K = x.shape
    N = wt_stack.shape[1]
    tm = _tile(T, tm)
    tn = _tile(N, tn)
    in_specs = [
        pl.BlockSpec((None, tn, K), lambda i, j: (layer, j, 0)),
        pl.BlockSpec((tm, K), lambda i, j: (i, 0)),
    ]
    args = [wt_stack, x]
    blocks = _nbytes((tm, K), BF16) + _nbytes((K, tn), BF16) + _nbytes((tm, tn), BF16)
    if rope is None:
        body = _proj_t_kernel
    else:
        assert seq % tm == 0
        nsb = seq // tm
        tab_spec = pl.BlockSpec((HEAD_DIM, tm), lambda i, j: (0, i % nsb))
        in_specs += [tab_spec, tab_spec]
        args += list(rope)
        blocks += 2 * _nbytes((tm, HEAD_DIM), F32)
        body = _proj_t_rope_kernel
    return pl.pallas_call(
        body,
        name=f"{name}_l{layer}",
        grid=(T // tm, N // tn),
        in_specs=in_specs,
        out_specs=pl.BlockSpec((tn, tm), lambda i, j: (j, i)),
        out_shape=jax.ShapeDtypeStruct((N, T), BF16),
        compiler_params=pltpu.CompilerParams(
            dimension_semantics=("parallel", "arbitrary"),
            vmem_limit_bytes=_vmem_limit(blocks, temp_bytes=2 * _nbytes((tm, tn), F32))),
    )(*args)


def _attn_kernel(lam_ref, g_ref, qt_ref, k_ref, vt_ref, o_ref, m_ref, l_ref, acc_ref,
                 sa_ref, sb_ref, xa_ref, xb_ref, *, tq, nq, lam_init):
    lv = lam_ref[...]
    lam = (jnp.exp(jnp.sum(lv[0:1] * lv[1:2], axis=-1, keepdims=True))
           - jnp.exp(jnp.sum(lv[2:3] * lv[3:4], axis=-1, keepdims=True)) + lam_init)
    gain = g_ref[...] * (1.0 - lam_init)

    def scores_into(s_ref, x_ref, qi, kj, diagonal):
        kblk = k_ref[pl.ds(pl.multiple_of(kj * tq, tq), tq), :]
        qcols = pl.ds(pl.multiple_of(qi * tq, tq), tq)
        for c in range(2):
            st = jnp.dot(kblk[:, c * HEAD_DIM:(c + 1) * HEAD_DIM],
                         qt_ref[c * HEAD_DIM:(c + 1) * HEAD_DIM, qcols], preferred_element_type=F32)
            if diagonal:
                key = lax.broadcasted_iota(jnp.int32, st.shape, 0)
                qry = lax.broadcasted_iota(jnp.int32, st.shape, 1)
                st = jnp.where(qry >= key, st, -jnp.inf)
            s_ref[c] = st
            x_ref[c] = jnp.max(st, axis=0, keepdims=True)

    def softmax_pv(s_ref, x_ref, kj):
        vtblk = vt_ref[:, pl.ds(pl.multiple_of(kj * tq, tq), tq)]
        for c in range(2):
            m_old = m_ref[c]
            m_new = jnp.maximum(m_old, x_ref[c])
            alpha = jnp.exp2(m_old - m_new)
            pt = jnp.exp2(s_ref[c] - m_new)
            l_ref[c] = alpha * l_ref[c] + jnp.sum(pt, axis=0, keepdims=True)
            acc_ref[c] = alpha * acc_ref[c] + jnp.dot(vtblk, pt.astype(BF16), preferred_element_type=F32)
            m_ref[c] = m_new

    A = (sa_ref, xa_ref)
    B = (sb_ref, xb_ref)

    def query_block(qi, carry):
        m_ref[...] = jnp.full(m_ref.shape, -jnp.inf, F32)
        l_ref[...] = jnp.zeros(l_ref.shape, F32)
        acc_ref[...] = jnp.zeros(acc_ref.shape, F32)

        def round_of_two(r, c2):
            scores_into(*B, qi, 2 * r + 1, False)
            softmax_pv(*A, 2 * r)
            scores_into(*A, qi, 2 * r + 2, False)
            softmax_pv(*B, 2 * r + 1)
            return c2

        lax.fori_loop(0, jnp.maximum(qi - 1, 0) // 2, round_of_two, 0)

        @pl.when(qi == 0)
        def _():
            softmax_pv(*A, 0)

        @pl.when(qi % 2 == 1)
        def _():
            scores_into(*B, qi, qi, True)
            softmax_pv(*A, qi - 1)
            softmax_pv(*B, qi)

        @pl.when(jnp.logical_and(qi % 2 == 0, qi > 0))
        def _():
            scores_into(*B, qi, qi - 1, False)
            softmax_pv(*A, qi - 2)
            scores_into(*A, qi, qi, True)
            softmax_pv(*B, qi - 1)
            softmax_pv(*A, qi)

        scores_into(*A, jnp.minimum(qi + 1, nq - 1), 0, False)
        ot = acc_ref[0] * (1.0 / l_ref[0]) - lam * (acc_ref[1] * (1.0 / l_ref[1]))
        ms = jnp.mean(ot * ot, axis=0, keepdims=True)
        ot = ot * lax.rsqrt(ms + LN_EPS)
        o_ref[pl.ds(pl.multiple_of(qi * tq, tq), tq), :] = (ot.T * gain).astype(o_ref.dtype)
        return carry

    scores_into(*A, 0, 0, True)
    lax.fori_loop(0, nq, query_block, 0)


def _diff_attention(qt, k, vt, lam_params, subln_g, layer, lam_init, *, batch, seq, tq):
    T, W = k.shape
    hw = 2 * HEAD_DIM
    heads = W // hw
    tq = _tile(seq, tq)
    nq = seq // tq
    blocks = 4 * _nbytes((seq, hw), BF16)
    scratch = 2 * _nbytes((tq, hw), F32) + 8 * _nbytes((SUBLANES, tq), F32) + 4 * _nbytes((tq, tq), F32)
    return pl.pallas_call(
        functools.partial(_attn_kernel, tq=tq, nq=nq, lam_init=lam_init),
        name=f"diff_attn_l{layer}",
        grid=(batch, heads),
        in_specs=[
            pl.BlockSpec((None, 4, HEAD_DIM), lambda b, h: (layer, 0, 0)),
            pl.BlockSpec((None, 1, hw), lambda b, h: (layer, 0, 0)),
            pl.BlockSpec((hw, seq), lambda b, h: (h, b)),
            pl.BlockSpec((seq, hw), lambda b, h: (b, h)),
            pl.BlockSpec((hw, seq), lambda b, h: (h, b)),
        ],
        out_specs=pl.BlockSpec((seq, hw), lambda b, h: (b, h)),
        out_shape=jax.ShapeDtypeStruct((T, W), BF16),
        scratch_shapes=[
            pltpu.VMEM((2, 1, tq), F32),
            pltpu.VMEM((2, 1, tq), F32),
            pltpu.VMEM((2, hw, tq), F32),
            pltpu.VMEM((2, tq, tq), F32),
            pltpu.VMEM((2, tq, tq), F32),
            pltpu.VMEM((2, 1, tq), F32),
            pltpu.VMEM((2, 1, tq), F32),
        ],
        compiler_params=pltpu.CompilerParams(
            dimension_semantics=("parallel", "parallel"),
            vmem_limit_bytes=_vmem_limit(blocks, scratch, 6 * _nbytes((tq, tq), F32))),
    )(lam_params, subln_g, qt, k, vt)


def _rnn_kernel(xr_ref, gr_ref, cw_ref, cb_ref, wrg_ref, brg_ref, lam_ref, o_ref,
                xprev, a_s, b_s, h_s, hcar, *, tt, nblk):
    t = pl.program_id(1)
    C = xr_ref.shape[1]
    bw = C // nblk

    @pl.when(t == 0)
    def _():
        xprev[...] = jnp.zeros(xprev.shape, F32)
        hcar[...] = jnp.zeros(hcar.shape, F32)

    ng = tt // SUBLANES
    x3 = xr_ref[...].reshape(ng, SUBLANES, C)
    tail = xprev[...]
    xprev[...] = x3[ng - 1]
    cw = cw_ref[...]
    sub_c = lax.broadcasted_iota(jnp.int32, (ng, SUBLANES, C), 1)
    xc3 = cw[CONV_WIDTH - 1:CONV_WIDTH] * x3 + cb_ref[...]
    for d in range(1, CONV_WIDTH):
        rolled = pltpu.roll(x3, d, axis=1)
        rolled_prev = jnp.concatenate([pltpu.roll(tail, d, axis=0)[None], rolled[:ng - 1]], axis=0)
        xc3 = xc3 + cw[CONV_WIDTH - 1 - d:CONV_WIDTH - d] * jnp.where(sub_c >= d, rolled, rolled_prev)
    xc = xc3.reshape(tt, C)

    sub = lax.broadcasted_iota(jnp.int32, (tt // SUBLANES, SUBLANES, bw), 1)
    for n in range(nblk):
        cs = slice(n * bw, (n + 1) * bw)
        xcn = xc[:, cs]
        xb = xcn.astype(BF16)
        gr_pre = jnp.dot(xb, wrg_ref[0, n], preferred_element_type=F32) + brg_ref[0:1, cs]
        gi_pre = jnp.dot(xb, wrg_ref[1, n], preferred_element_type=F32) + brg_ref[1:2, cs]
        i = _sigmoid(gi_pre)
        half_rate = (-0.5 * LRU_C) * jax.nn.softplus(-lam_ref[:, cs])
        log_a = half_rate * jnp.tanh(0.5 * gr_pre) + half_rate
        a = jnp.exp(log_a)
        t = (1.0 - a) * (1.0 + a)
        u = jnp.where(t > 0.0, t * lax.rsqrt(t), 0.0) * (i * xcn)
        a3 = a.reshape(tt // SUBLANES, SUBLANES, bw)
        u3 = u.reshape(tt // SUBLANES, SUBLANES, bw)
        for s in (1, 2, 4):
            keep = sub >= s
            u_new = a3 * pltpu.roll(u3, s, axis=1) + u3
            a_new = a3 * pltpu.roll(a3, s, axis=1)
            u3 = jnp.where(keep, u_new, u3)
            a3 = jnp.where(keep, a_new, a3)
        a_s[:, cs] = a3.reshape(tt, bw)
        b_s[:, cs] = u3.reshape(tt, bw)

    def group(j, h):
        rows = pl.ds(pl.multiple_of(j * SUBLANES, SUBLANES), SUBLANES)
        hj = a_s[rows, :] * h + b_s[rows, :]
        h_s[rows, :] = hj
        return jnp.broadcast_to(hj[SUBLANES - 1:SUBLANES, :], hj.shape)

    hcar[...] = lax.fori_loop(0, tt // SUBLANES, group, hcar[...])
    o_ref[...] = (h_s[...] * _gelu_tanh(gr_ref[...])).astype(o_ref.dtype)


def _rglru(rest, conv_w, conv_b, w_rg, b_rg, lru_lambda, layer, *, batch, seq, width, tt):
    T = rest.shape[0]
    nblk = w_rg.shape[2]
    tt = _tile(seq, tt)
    nt = seq // tt
    blocks = 2 * _nbytes((tt, width), F32) + _nbytes((tt, width), BF16) + _nbytes(w_rg.shape[1:], BF16)
    scratch = 4 * _nbytes((tt + SUBLANES, width), F32)
    return pl.pallas_call(
        functools.partial(_rnn_kernel, tt=tt, nblk=nblk),
        name=f"rglru_l{layer}",
        grid=(batch, nt),
        in_specs=[
            pl.BlockSpec((tt, width), lambda b, t: (b * nt + t, 0)),
            pl.BlockSpec((tt, width), lambda b, t: (b * nt + t, 1)),
            pl.BlockSpec((None, CONV_WIDTH, width), lambda b, t: (layer, 0, 0)),
            pl.BlockSpec((None, 1, width), lambda b, t: (layer, 0, 0)),
            pl.BlockSpec((None,) + w_rg.shape[1:], lambda b, t: (layer, 0, 0, 0, 0)),
            pl.BlockSpec((None, 2, width), lambda b, t: (layer, 0, 0)),
            pl.BlockSpec((None, 1, width), lambda b, t: (layer, 0, 0)),
        ],
        out_specs=pl.BlockSpec((tt, width), lambda b, t: (b * nt + t, 0)),
        out_shape=jax.ShapeDtypeStruct((T, width), BF16),
        scratch_shapes=[
            pltpu.VMEM((SUBLANES, width), F32),
            pltpu.VMEM((tt, width), F32),
            pltpu.VMEM((tt, width), F32),
            pltpu.VMEM((tt, width), F32),
            pltpu.VMEM((SUBLANES, width), F32),
        ],
        compiler_params=pltpu.CompilerParams(
            dimension_semantics=("parallel", "arbitrary"),
            vmem_limit_bytes=_vmem_limit(blocks, scratch, 6 * _nbytes((tt, width), F32))),
    )(rest, rest, conv_w, conv_b, w_rg, b_rg, lru_lambda)


def _merge_kernel(oa_ref, hr_ref, wa_ref, wr_ref, ga_ref, gr_ref, bm_ref, o_ref):
    ya = jnp.dot(oa_ref[...], wa_ref[...], preferred_element_type=F32)
    yr = jnp.dot(hr_ref[...], wr_ref[...], preferred_element_type=F32)
    ga = _sigmoid(ga_ref[...] + bm_ref[0:1, :])
    gr = _sigmoid(gr_ref[...] + bm_ref[1:2, :])
    o_ref[...] = (ga * ya + gr * yr).astype(o_ref.dtype)


def _merge(o_attn, h_rnn, w_branch, rest, b_merge, layer, *, tm, tn):
    T, K = o_attn.shape
    D = w_branch.shape[-1]
    tm = _tile(T, tm)
    tn = _tile(D, tn)
    nb = D // tn
    gate0 = (rest.shape[1] - 2 * D) // tn
    blocks = (2 * _nbytes((tm, K), BF16) + 2 * _nbytes((K, tn), BF16) + 2 * _nbytes((tm, tn), F32)
              + _nbytes((tm, tn), BF16))
    return pl.pallas_call(
        _merge_kernel,
        name=f"merge_l{layer}",
        grid=(T // tm, nb),
        in_specs=[
            pl.BlockSpec((tm, K), lambda i, j: (i, 0)),
            pl.BlockSpec((tm, K), lambda i, j: (i, 0)),
            pl.BlockSpec((None, None, K, tn), lambda i, j: (layer, 0, 0, j)),
            pl.BlockSpec((None, None, K, tn), lambda i, j: (layer, 1, 0, j)),
            pl.BlockSpec((tm, tn), lambda i, j: (i, gate0 + j)),
            pl.BlockSpec((tm, tn), lambda i, j: (i, gate0 + nb + j)),
            pl.BlockSpec((None, 2, tn), lambda i, j: (layer, 0, j)),
        ],
        out_specs=pl.BlockSpec((tm, tn), lambda i, j: (i, j)),
        out_shape=jax.ShapeDtypeStruct((T, D), BF16),
        compiler_params=pltpu.CompilerParams(
            dimension_semantics=("parallel", "arbitrary"),
            vmem_limit_bytes=_vmem_limit(blocks, temp_bytes=4 * _nbytes((tm, tn), F32))),
    )(o_attn, h_rnn, w_branch, w_branch, rest, rest, b_merge)


def _layer_norm_rows(xf, g, b):
    mu = jnp.mean(xf, axis=-1, keepdims=True)
    d = xf - mu
    var = jnp.mean(d * d, axis=-1, keepdims=True)
    return d * lax.rsqrt(var + LN_EPS) * g + b


def _outproj_ln_kernel(m_ref, w_ref, x_ref, g_ref, b_ref, of_ref, ob_ref, *, alpha, which):
    mix = jnp.dot(m_ref[...], w_ref[...], preferred_element_type=F32)
    y = _layer_norm_rows(alpha * x_ref[...] + mix, g_ref[which:which + 1, :], b_ref[which:which + 1, :])
    of_ref[...] = y
    ob_ref[...] = y.astype(BF16)


def _outproj_ln(merged, w_out, x, ln_g, ln_b, layer, alpha, *, tm):
    T, K = merged.shape
    D = w_out.shape[-1]
    tm = _tile(T, tm)
    blocks = (_nbytes((tm, K), BF16) + _nbytes((K, D), BF16) + 2 * _nbytes((tm, D), F32)
              + _nbytes((tm, D), BF16))
    return pl.pallas_call(
        functools.partial(_outproj_ln_kernel, alpha=alpha, which=0),
        name=f"outproj_ln_l{layer}",
        grid=(T // tm,),
        in_specs=[
            pl.BlockSpec((tm, K), lambda i: (i, 0)),
            pl.BlockSpec((None, K, D), lambda i: (layer, 0, 0)),
            pl.BlockSpec((tm, D), lambda i: (i, 0)),
            pl.BlockSpec((None, 2, D), lambda i: (layer, 0, 0)),
            pl.BlockSpec((None, 2, D), lambda i: (layer, 0, 0)),
        ],
        out_specs=[pl.BlockSpec((tm, D), lambda i: (i, 0)), pl.BlockSpec((tm, D), lambda i: (i, 0))],
        out_shape=[jax.ShapeDtypeStruct((T, D), F32), jax.ShapeDtypeStruct((T, D), BF16)],
        compiler_params=pltpu.CompilerParams(
            dimension_semantics=("parallel",),
            vmem_limit_bytes=_vmem_limit(blocks, temp_bytes=3 * _nbytes((tm, D), F32))),
    )(merged, w_out, x, ln_g, ln_b)


def _ffn_ln_kernel(xb_ref, wg_ref, wu_ref, wd_ref, x_ref, g_ref, b_ref, of_ref, ob_ref, acc_ref, *, alpha):
    f = pl.program_id(1)

    @pl.when(f == 0)
    def _():
        acc_ref[...] = jnp.zeros(acc_ref.shape, F32)

    xb = xb_ref[...]
    hg = jnp.dot(xb, wg_ref[...], preferred_element_type=F32)
    hu = jnp.dot(xb, wu_ref[...], preferred_element_type=F32)
    act = (hg * _sigmoid(hg) * hu).astype(BF16)
    acc_ref[...] += jnp.dot(act, wd_ref[...], preferred_element_type=F32)

    @pl.when(f == pl.num_programs(1) - 1)
    def _():
        y = _layer_norm_rows(alpha * x_ref[...] + acc_ref[...], g_ref[1:2, :], b_ref[1:2, :])
        of_ref[...] = y
        ob_ref[...] = y.astype(BF16)


def _ffn_ln(xb, x, w_gate_up, w_down, ln_g, ln_b, layer, alpha, *, tm, tf):
    T, D = x.shape
    F = w_down.shape[1]
    tm = _tile(T, tm)
    assert F % tf == 0
    nf = F // tf
    blocks = (_nbytes((tm, D), BF16) + 3 * _nbytes((D, tf), BF16) + 2 * _nbytes((tm, D), F32)
              + _nbytes((tm, D), BF16))
    return pl.pallas_call(
        functools.partial(_ffn_ln_kernel, alpha=alpha),
        name=f"ffn_ln_l{layer}",
        grid=(T // tm, nf),
        in_specs=[
            pl.BlockSpec((tm, D), lambda i, f: (i, 0)),
            pl.BlockSpec((None, D, tf), lambda i, f: (layer, 0, f)),
            pl.BlockSpec((None, D, tf), lambda i, f: (layer, 0, nf + f)),
            pl.BlockSpec((None, tf, D), lambda i, f: (layer, f, 0)),
            pl.BlockSpec((tm, D), lambda i, f: (i, 0)),
            pl.BlockSpec((None, 2, D), lambda i, f: (layer, 0, 0)),
            pl.BlockSpec((None, 2, D), lambda i, f: (layer, 0, 0)),
        ],
        out_specs=[pl.BlockSpec((tm, D), lambda i, f: (i, 0)), pl.BlockSpec((tm, D), lambda i, f: (i, 0))],
        out_shape=[jax.ShapeDtypeStruct((T, D), F32), jax.ShapeDtypeStruct((T, D), BF16)],
        scratch_shapes=[pltpu.VMEM((tm, D), F32)],
        compiler_params=pltpu.CompilerParams(
            dimension_semantics=("parallel", "arbitrary"),
            vmem_limit_bytes=_vmem_limit(blocks, _nbytes((tm, D), F32),
                                         3 * _nbytes((tm, tf), F32) + _nbytes((tm, D), F32))),
    )(xb, w_gate_up, w_gate_up, w_down, x, ln_g, ln_b)


def _rope_tables(seq):
    half = HEAD_DIM // 2
    inv_freq = ROPE_THETA ** (-jnp.arange(half, dtype=F32) * 2.0 / HEAD_DIM)
    ang = jnp.arange(seq, dtype=F32)[:, None] * inv_freq[None, :]
    ang = jnp.concatenate([ang, ang], axis=-1)
    sign = jnp.concatenate([-jnp.ones((half,), F32), jnp.ones((half,), F32)])
    return jnp.cos(ang), jnp.sin(ang) * sign


def kernel(x, w_in, b_merge, diff_lambda, subln_g, conv_w, conv_b, w_rg, b_rg, lru_lambda,
           w_branch, w_out, ln_g, ln_b, w_gate_up, w_down):
    B, S, D = x.shape
    T = B * S
    depth = w_in.shape[0]
    attn_w = w_branch.shape[2]
    d_rnn = conv_w.shape[-1]
    qk_w = (w_in.shape[-1] - attn_w - 2 * d_rnn - 2 * D) // 2
    alpha = (2.0 * depth) ** 0.25

    w_in_b = w_in.astype(BF16)
    wq_t = jnp.swapaxes(w_in[:, :, :qk_w], 1, 2).astype(BF16)
    wv_t = jnp.swapaxes(w_in[:, :, 2 * qk_w:2 * qk_w + attn_w], 1, 2).astype(BF16)
    w_rg_b = w_rg.astype(BF16)
    w_branch_b = w_branch.astype(BF16)
    w_out_b = w_out.astype(BF16)
    w_gate_up_b = w_gate_up.astype(BF16)
    w_down_b = w_down.astype(BF16)
    subln_g3 = subln_g.reshape(depth, 1, -1)
    conv_b3 = conv_b.reshape(depth, 1, -1)
    lru_lambda3 = lru_lambda.reshape(depth, 1, -1)
    cos_k, sin_k = _rope_tables(S)
    q_scale = HEAD_DIM ** -0.5 * math.log2(math.e)
    rope_q_t = (cos_k.T * q_scale, sin_k.T * q_scale)
    rope_k = (cos_k[None], sin_k[None])

    xf = x.reshape(T, D)
    xb = xf.astype(BF16)
    for l in range(depth):
        lam_init = 0.8 - 0.6 * math.exp(-0.3 * l)
        qt = _project_t(xb, wq_t, l, "proj_qt", tm=1024, tn=2048, rope=rope_q_t, seq=S)
        k = _project(xb, w_in_b, l, qk_w, qk_w, BF16, tm=1024, tn=2048, rope=rope_k, seq=S)
        vt = _project_t(xb, wv_t, l, "proj_vt", tm=1024, tn=2048)
        rest = _project(xb, w_in_b, l, 2 * qk_w + attn_w, 2 * d_rnn + 2 * D, F32, tm=1024, tn=2048)
        o_attn = _diff_attention(qt, k, vt, diff_lambda, subln_g3, l, lam_init,
                                 batch=B, seq=S, tq=512)
        h_rnn = _rglru(rest, conv_w, conv_b3, w_rg_b, b_rg, lru_lambda3, l,
                       batch=B, seq=S, width=d_rnn, tt=512)
        merged = _merge(o_attn, h_rnn, w_branch_b, rest, b_merge, l, tm=1024, tn=512)
        xf, xb = _outproj_ln(merged, w_out_b, xf, ln_g, ln_b, l, alpha, tm=512)
        xf, xb = _ffn_ln(xb, xf, w_gate_up_b, w_down_b, ln_g, ln_b, l, alpha, tm=512, tf=FFN_CHUNK)
    return xf.reshape(B, S, D)
```

```python
import functools
import math

import jax
import jax.numpy as jnp
from jax import lax
from jax.experimental import pallas as pl
from jax.experimental.pallas import tpu as pltpu

HEAD_DIM = 128
ROPE_THETA = 10000.0
CONV_WIDTH = 4
LRU_C = 8.0
LN_EPS = 1e-5

LANES = 128
SUBLANES = 8
V7X_VMEM_BYTES = 64 * 1024 * 1024
V7X_VMEM_CEILING = V7X_VMEM_BYTES - 6 * 1024 * 1024

BF16 = jnp.bfloat16
F32 = jnp.float32

FFN_CHUNK = 512


def _vmem_limit(pipelined_bytes, resident_bytes=0, temp_bytes=0):
    need = 2 * pipelined_bytes + resident_bytes + temp_bytes + (4 << 20)
    return int(min(max(need, 16 << 20), V7X_VMEM_CEILING))


def _nbytes(shape, dtype):
    return math.prod(shape) * jnp.dtype(dtype).itemsize


def _sigmoid(x):
    return 0.5 * jnp.tanh(0.5 * x) + 0.5


def _gelu_tanh(x):
    c = math.sqrt(2.0 / math.pi)
    inner = x * (c + (c * 0.044715) * (x * x))
    return x * (0.5 * jnp.tanh(inner) + 0.5)


def _tile(dim, want):
    t = min(dim, want)
    while dim % t:
        t //= 2
    return t


def _proj_kernel(x_ref, w_ref, o_ref):
    acc = jnp.dot(x_ref[...], w_ref[...], preferred_element_type=F32)
    o_ref[...] = acc.astype(o_ref.dtype)


def _proj_rope_kernel(x_ref, w_ref, cos_ref, sin_ref, o_ref):
    acc = jnp.dot(x_ref[...], w_ref[...], preferred_element_type=F32)
    cos = cos_ref[...]
    sin = sin_ref[...]
    for c in range(acc.shape[1] // HEAD_DIM):
        t = acc[:, c * HEAD_DIM:(c + 1) * HEAD_DIM]
        rot = pltpu.roll(t, HEAD_DIM // 2, axis=1)
        o_ref[:, c * HEAD_DIM:(c + 1) * HEAD_DIM] = (t * cos + rot * sin).astype(o_ref.dtype)


def _project(x, w_stack, layer, col0, ncols, out_dtype, *, tm, tn, rope=None, seq=None):
    T, K = x.shape
    tm = _tile(T, tm)
    tn = _tile(ncols, tn)
    assert col0 % tn == 0
    cb0 = col0 // tn
    grid = (T // tm, ncols // tn)
    in_specs = [
        pl.BlockSpec((tm, K), lambda i, j: (i, 0)),
        pl.BlockSpec((None, K, tn), lambda i, j: (layer, 0, j + cb0)),
    ]
    args = [x, w_stack]
    blocks = _nbytes((tm, K), BF16) + _nbytes((K, tn), BF16) + _nbytes((tm, tn), out_dtype)
    if rope is None:
        body = _proj_kernel
    else:
        cos_t, sin_t = rope
        groups = cos_t.shape[0]
        per = (ncols // tn) // groups
        tms = _tile(seq, tm)
        assert tms == tm
        nsb = seq // tm
        tab_spec = pl.BlockSpec((None, tm, HEAD_DIM), lambda i, j: (j // per, i % nsb, 0))
        in_specs += [tab_spec, tab_spec]
        args += [cos_t, sin_t]
        blocks += 2 * _nbytes((tm, HEAD_DIM), F32)
        body = _proj_rope_kernel
    return pl.pallas_call(
        body,
        name=f"proj_c{col0}_l{layer}",
        grid=grid,
        in_specs=in_specs,
        out_specs=pl.BlockSpec((tm, tn), lambda i, j: (i, j)),
        out_shape=jax.ShapeDtypeStruct((T, ncols), out_dtype),
        compiler_params=pltpu.CompilerParams(
            dimension_semantics=("parallel", "arbitrary"),
            vmem_limit_bytes=_vmem_limit(blocks, temp_bytes=2 * _nbytes((tm, tn), F32))),
    )(*args)


_NT = (((1,), (1,)), ((), ()))


def _proj_t_kernel(w_ref, x_ref, o_ref):
    acc = lax.dot_general(w_ref[...], x_ref[...], _NT, preferred_element_type=F32)
    o_ref[...] = acc.astype(o_ref.dtype)


def _proj_t_rope_kernel(w_ref, x_ref, cos_ref, sin_ref, o_ref):
    acc = lax.dot_general(w_ref[...], x_ref[...], _NT, preferred_element_type=F32)
    cos = cos_ref[...]
    sin = sin_ref[...]
    half = HEAD_DIM // 2
    for c in range(acc.shape[0] // HEAD_DIM):
        t = acc[c * HEAD_DIM:(c + 1) * HEAD_DIM, :]
        rot = jnp.concatenate([t[half:], t[:half]], axis=0)
        o_ref[c * HEAD_DIM:(c + 1) * HEAD_DIM, :] = (t * cos + rot * sin).astype(o_ref.dtype)


def _project_t(x, wt_stack, layer, name, *, tm, tn, rope=None, seq=None):
    T, K = x.shape
    N = wt_stack.shape[1]
    tm = _tile(T, tm)
    tn = _tile(N, tn)
    in_specs = [
        pl.BlockSpec((None, tn, K), lambda i, j: (layer, j, 0)),
        pl.BlockSpec((tm, K), lambda i, j: (i, 0)),
    ]
    args = [wt_stack, x]
    blocks = _nbytes((tm, K), BF16) + _nbytes((K, tn), BF16) + _nbytes((tm, tn), BF16)
    if rope is None:
        body = _proj_t_kernel
    else:
        assert seq % tm == 0
        nsb = seq // tm
        tab_spec = pl.BlockSpec((HEAD_DIM, tm), lambda i, j: (0, i % nsb))
        in_specs += [tab_spec, tab_spec]
        args += list(rope)
        blocks += 2 * _nbytes((tm, HEAD_DIM), F32)
        body = _proj_t_rope_kernel
    return pl.pallas_call(
        body,
        name=f"{name}_l{layer}",
        grid=(T // tm, N // tn),
        in_specs=in_specs,
        out_specs=pl.BlockSpec((tn, tm), lambda i, j: (j, i)),
        out_shape=jax.ShapeDtypeStruct((N, T), BF16),
        compiler_params=pltpu.CompilerParams(
            dimension_semantics=("parallel", "arbitrary"),
            vmem_limit_bytes=_vmem_limit(blocks, temp_bytes=2 * _nbytes((tm, tn), F32))),
    )(*args)


def _attn_kernel(lam_ref, g_ref, qt_ref, k_ref, vt_ref, o_ref, m_ref, l_ref, acc_ref,
                 sa_ref, sb_ref, xa_ref, xb_ref, *, tq, nq, lam_init):
    lv = lam_ref[...]
    lam = (jnp.exp(jnp.sum(lv[0:1] * lv[1:2], axis=-1, keepdims=True))
           - jnp.exp(jnp.sum(lv[2:3] * lv[3:4], axis=-1, keepdims=True)) + lam_init)
    gain = g_ref[...] * (1.0 - lam_init)

    def scores_into(s_ref, x_ref, qi, kj, diagonal):
        kblk = k_ref[pl.ds(pl.multiple_of(kj * tq, tq), tq), :]
        qcols = pl.ds(pl.multiple_of(qi * tq, tq), tq)
        for c in range(2):
            st = jnp.dot(kblk[:, c * HEAD_DIM:(c + 1) * HEAD_DIM],
                         qt_ref[c * HEAD_DIM:(c + 1) * HEAD_DIM, qcols], preferred_element_type=F32)
            if diagonal:
                key = lax.broadcasted_iota(jnp.int32, st.shape, 0)
                qry = lax.broadcasted_iota(jnp.int32, st.shape, 1)
                st = jnp.where(qry >= key, st, -jnp.inf)
            s_ref[c] = st
            x_ref[c] = jnp.max(st, axis=0, keepdims=True)

    def softmax_pv(s_ref, x_ref, kj):
        vtblk = vt_ref[:, pl.ds(pl.multiple_of(kj * tq, tq), tq)]
        for c in range(2):
            m_old = m_ref[c]
            m_new = jnp.maximum(m_old, x_ref[c])
            alpha = jnp.exp2(m_old - m_new)
            pt = jnp.exp2(s_ref[c] - m_new)
            l_ref[c] = alpha * l_ref[c] + jnp.sum(pt, axis=0, keepdims=True)
            acc_ref[c] = alpha * acc_ref[c] + jnp.dot(vtblk, pt.astype(BF16), preferred_element_type=F32)
            m_ref[c] = m_new

    A = (sa_ref, xa_ref)
    B = (sb_ref, xb_ref)

    def query_block(qi, carry):
        m_ref[...] = jnp.full(m_ref.shape, -jnp.inf, F32)
        l_ref[...] = jnp.zeros(l_ref.shape, F32)
        acc_ref[...] = jnp.zeros(acc_ref.shape, F32)

        def round_of_two(r, c2):
            scores_into(*B, qi, 2 * r + 1, False)
            softmax_pv(*A, 2 * r)
            scores_into(*A, qi, 2 * r + 2, False)
            softmax_pv(*B, 2 * r + 1)
            return c2

        lax.fori_loop(0, jnp.maximum(qi - 1, 0) // 2, round_of_two, 0)

        @pl.when(qi == 0)
        def _():
            softmax_pv(*A, 0)

        @pl.when(qi % 2 == 1)
        def _():
            scores_into(*B, qi, qi, True)
            softmax_pv(*A, qi - 1)
            softmax_pv(*B, qi)

        @pl.when(jnp.logical_and(qi % 2 == 0, qi > 0))
        def _():
            scores_into(*B, qi, qi - 1, False)
            softmax_pv(*A, qi - 2)
            scores_into(*A, qi, qi, True)
            softmax_pv(*B, qi - 1)
            softmax_pv(*A, qi)

        scores_into(*A, jnp.minimum(qi + 1, nq - 1), 0, False)
        ot = acc_ref[0] * (1.0 / l_ref[0]) - lam * (acc_ref[1] * (1.0 / l_ref[1]))
        ms = jnp.mean(ot * ot, axis=0, keepdims=True)
        ot = ot * lax.rsqrt(ms + LN_EPS)
        o_ref[pl.ds(pl.multiple_of(qi * tq, tq), tq), :] = (ot.T * gain).astype(o_ref.dtype)
        return carry

    scores_into(*A, 0, 0, True)
    lax.fori_loop(0, nq, query_block, 0)


def _diff_attention(qt, k, vt, lam_params, subln_g, layer, lam_init, *, batch, seq, tq):
    T, W = k.shape
    hw = 2 * HEAD_DIM
    heads = W // hw
    tq = _tile(seq, tq)
    nq = seq // tq
    blocks = 4 * _nbytes((seq, hw), BF16)
    scratch = 2 * _nbytes((tq, hw), F32) + 8 * _nbytes((SUBLANES, tq), F32) + 4 * _nbytes((tq, tq), F32)
    return pl.pallas_call(
        functools.partial(_attn_kernel, tq=tq, nq=nq, lam_init=lam_init),
        name=f"diff_attn_l{layer}",
        grid=(batch, heads),
        in_specs=[
            pl.BlockSpec((None, 4, HEAD_DIM), lambda b, h: (layer, 0, 0)),
            pl.BlockSpec((None, 1, hw), lambda b, h: (layer, 0, 0)),
            pl.BlockSpec((hw, seq), lambda b, h: (h, b)),
            pl.BlockSpec((seq, hw), lambda b, h: (b, h)),
            pl.BlockSpec((hw, seq), lambda b, h: (h, b)),
        ],
        out_specs=pl.BlockSpec((seq, hw), lambda b, h: (b, h)),
        out_shape=jax.ShapeDtypeStruct((T, W), BF16),
        scratch_shapes=[
            pltpu.VMEM((2, 1, tq), F32),
            pltpu.VMEM((2, 1, tq), F32),
            pltpu.VMEM((2, hw, tq), F32),
            pltpu.VMEM((2, tq, tq), F32),
            pltpu.VMEM((2, tq, tq), F32),
            pltpu.VMEM((2, 1, tq), F32),
            pltpu.VMEM((2, 1, tq), F32),
        ],
        compiler_params=pltpu.CompilerParams(
            dimension_semantics=("parallel", "parallel"),
            vmem_limit_bytes=_vmem_limit(blocks, scratch, 6 * _nbytes((tq, tq), F32))),
    )(lam_params, subln_g, qt, k, vt)


def _rows_down_one(y, first_row):
    n, C = y.shape
    y3 = y.reshape(n // SUBLANES, SUBLANES, C)
    rolled = pltpu.roll(y3, 1, axis=1)
    lead = jnp.broadcast_to(first_row, (SUBLANES, C))[None]
    prev = jnp.concatenate([lead, rolled[:n // SUBLANES - 1]], axis=0)
    sub = lax.broadcasted_iota(jnp.int32, y3.shape, 1)
    return jnp.where(sub >= 1, rolled, prev).reshape(n, C)


def _segment_end_scan(q, e, h0):
    n, C = q.shape
    ng = n // SUBLANES
    q3 = q.reshape(ng, SUBLANES, C)
    e3 = e.reshape(ng, SUBLANES, C)
    sub = lax.broadcasted_iota(jnp.int32, q3.shape, 1)
    for s in (1, 2, 4):
        keep = sub >= s
        e_new = q3 * pltpu.roll(e3, s, axis=1) + e3
        q_new = q3 * pltpu.roll(q3, s, axis=1)
        e3 = jnp.where(keep, e_new, e3)
        q3 = jnp.where(keep, q_new, q3)
    groups = []
    carry = jnp.broadcast_to(h0, (SUBLANES, C))
    for g in range(ng):
        fg = q3[g] * carry + e3[g]
        groups.append(fg)
        carry = jnp.broadcast_to(fg[SUBLANES - 1:SUBLANES, :], fg.shape)
    return jnp.concatenate(groups, axis=0)


def _rnn_kernel(xr_ref, gr_ref, cw_ref, cb_ref, wrg_ref, brg_ref, lam_ref, o_ref,
                xs, hs, xlast, hcar, *, tt, nblk):
    t = pl.program_id(1)
    C = xr_ref.shape[1]
    bw = C // nblk
    nslab = C // LANES
    nseg = tt // SUBLANES
    P = SUBLANES

    @pl.when(t == 0)
    def _():
        xlast[...] = jnp.zeros(xlast.shape, F32)
        hcar[...] = jnp.zeros(hcar.shape, F32)

    for n in range(nslab):
        xs[n] = xr_ref[:, n * LANES:(n + 1) * LANES]

    def phase(k):
        return jnp.concatenate([xs[n, pl.ds(k, nseg, stride=P), :] for n in range(nslab)], axis=1)

    xp = [phase(k) for k in range(P)]
    prev_rows = xlast[...]
    xlast[...] = xr_ref[tt - P:tt, :]
    xp_down = {k: _rows_down_one(xp[k], prev_rows[k:k + 1]) for k in range(P - CONV_WIDTH + 1, P)}

    cw = cw_ref[...]
    xc = []
    for k in range(P):
        acc = cw[CONV_WIDTH - 1:CONV_WIDTH] * xp[k] + cb_ref[...]
        for d in range(1, CONV_WIDTH):
            src = xp[k - d] if k >= d else xp_down[k - d + P]
            acc = acc + cw[CONV_WIDTH - 1 - d:CONV_WIDTH - d] * src
        xc.append(acc)
    xc_all = jnp.concatenate(xc, axis=0)

    for n in range(nblk):
        cs = slice(n * bw, (n + 1) * bw)
        xcn = xc_all[:, cs]
        xb = xcn.astype(BF16)
        tr = jnp.tanh(jnp.dot(xb, wrg_ref[0, n], preferred_element_type=F32) + brg_ref[0:1, cs])
        ti = jnp.tanh(jnp.dot(xb, wrg_ref[1, n], preferred_element_type=F32) + brg_ref[1:2, cs])
        i = 0.5 * ti + 0.5
        half_rate = (-0.5 * LRU_C * math.log2(math.e)) * jax.nn.softplus(-lam_ref[:, cs])
        a = jnp.exp2(half_rate * tr + half_rate)
        t1 = (1.0 - a) * (1.0 + a)
        u = jnp.where(t1 > 0.0, t1 * lax.rsqrt(t1), 0.0) * (i * xcn)

        hk = [u[0:nseg]]
        pk = [a[0:nseg]]
        for k in range(1, P):
            ak = a[k * nseg:(k + 1) * nseg]
            hk.append(ak * hk[-1] + u[k * nseg:(k + 1) * nseg])
            pk.append(ak * pk[-1])
        h0 = hcar[0:1, cs]
        f = _segment_end_scan(pk[P - 1], hk[P - 1], h0)
        hcar[:, cs] = jnp.broadcast_to(f[nseg - 1:nseg, :], (SUBLANES, bw))
        g = _rows_down_one(f, h0)
        for k in range(P):
            hk_true = hk[k] + pk[k] * g
            for j in range(bw // LANES):
                hs[n * (bw // LANES) + j, pl.ds(k, nseg, stride=P), :] = hk_true[:, j * LANES:(j + 1) * LANES]

    h = jnp.concatenate([hs[n] for n in range(nslab)], axis=1)
    o_ref[...] = (h * _gelu_tanh(gr_ref[...])).astype(o_ref.dtype)


def _rglru(rest, conv_w, conv_b, w_rg, b_rg, lru_lambda, layer, *, batch, seq, width, tt):
    T = rest.shape[0]
    nblk = w_rg.shape[2]
    tt = _tile(seq, tt)
    nt = seq // tt
    blocks = 2 * _nbytes((tt, width), F32) + _nbytes((tt, width), BF16) + _nbytes(w_rg.shape[1:], BF16)
    scratch = 2 * _nbytes((tt + 2 * SUBLANES, width), F32)
    return pl.pallas_call(
        functools.partial(_rnn_kernel, tt=tt, nblk=nblk),
        name=f"rglru_l{layer}",
        grid=(batch, nt),
        in_specs=[
            pl.BlockSpec((tt, width), lambda b, t: (b * nt + t, 0)),
            pl.BlockSpec((tt, width), lambda b, t: (b * nt + t, 1)),
            pl.BlockSpec((None, CONV_WIDTH, width), lambda b, t: (layer, 0, 0)),
            pl.BlockSpec((None, 1, width), lambda b, t: (layer, 0, 0)),
            pl.BlockSpec((None,) + w_rg.shape[1:], lambda b, t: (layer, 0, 0, 0, 0)),
            pl.BlockSpec((None, 2, width), lambda b, t: (layer, 0, 0)),
            pl.BlockSpec((None, 1, width), lambda b, t: (layer, 0, 0)),
        ],
        out_specs=pl.BlockSpec((tt, width), lambda b, t: (b * nt + t, 0)),
        out_shape=jax.ShapeDtypeStruct((T, width), BF16),
        scratch_shapes=[
            pltpu.VMEM((width // LANES, tt, LANES), F32),
            pltpu.VMEM((width // LANES, tt, LANES), F32),
            pltpu.VMEM((SUBLANES, width), F32),
            pltpu.VMEM((SUBLANES, width), F32),
        ],
        compiler_params=pltpu.CompilerParams(
            dimension_semantics=("parallel", "arbitrary"),
            vmem_limit_bytes=_vmem_limit(blocks, scratch, 6 * _nbytes((tt, width), F32))),
    )(rest, rest, conv_w, conv_b, w_rg, b_rg, lru_lambda)


def _merge_kernel(oa_ref, hr_ref, wa_ref, wr_ref, ga_ref, gr_ref, bm_ref, o_ref):
    ya = jnp.dot(oa_ref[...], wa_ref[...], preferred_element_type=F32)
    yr = jnp.dot(hr_ref[...], wr_ref[...], preferred_element_type=F32)
    ga = _sigmoid(ga_ref[...] + bm_ref[0:1, :])
    gr = _sigmoid(gr_ref[...] + bm_ref[1:2, :])
    o_ref[...] = (ga * ya + gr * yr).astype(o_ref.dtype)


def _merge(o_attn, h_rnn, w_branch, rest, b_merge, layer, *, tm, tn):
    T, K = o_attn.shape
    D = w_branch.shape[-1]
    tm = _tile(T, tm)
    tn = _tile(D, tn)
    nb = D // tn
    gate0 = (rest.shape[1] - 2 * D) // tn
    blocks = (2 * _nbytes((tm, K), BF16) + 2 * _nbytes((K, tn), BF16) + 2 * _nbytes((tm, tn), F32)
              + _nbytes((tm, tn), BF16))
    return pl.pallas_call(
        _merge_kernel,
        name=f"merge_l{layer}",
        grid=(T // tm, nb),
        in_specs=[
            pl.BlockSpec((tm, K), lambda i, j: (i, 0)),
            pl.BlockSpec((tm, K), lambda i, j: (i, 0)),
            pl.BlockSpec((None, None, K, tn), lambda i, j: (layer, 0, 0, j)),
            pl.BlockSpec((None, None, K, tn), lambda i, j: (layer, 1, 0, j)),
            pl.BlockSpec((tm, tn), lambda i, j: (i, gate0 + j)),
            pl.BlockSpec((tm, tn), lambda i, j: (i, gate0 + nb + j)),
            pl.BlockSpec((None, 2, tn), lambda i, j: (layer, 0, j)),
        ],
        out_specs=pl.BlockSpec((tm, tn), lambda i, j: (i, j)),
        out_shape=jax.ShapeDtypeStruct((T, D), BF16),
        compiler_params=pltpu.CompilerParams(
            dimension_semantics=("parallel", "arbitrary"),
            vmem_limit_bytes=_vmem_limit(blocks, temp_bytes=4 * _nbytes((tm, tn), F32))),
    )(o_attn, h_rnn, w_branch, w_branch, rest, rest, b_merge)


def _layer_norm_rows(xf, g, b):
    mu = jnp.mean(xf, axis=-1, keepdims=True)
    d = xf - mu
    var = jnp.mean(d * d, axis=-1, keepdims=True)
    return d * lax.rsqrt(var + LN_EPS) * g + b


def _outproj_ln_kernel(m_ref, w_ref, x_ref, g_ref, b_ref, of_ref, ob_ref, *, alpha, which):
    mix = jnp.dot(m_ref[...], w_ref[...], preferred_element_type=F32)
    y = _layer_norm_rows(alpha * x_ref[...] + mix, g_ref[which:which + 1, :], b_ref[which:which + 1, :])
    of_ref[...] = y
    ob_ref[...] = y.astype(BF16)


def _outproj_ln(merged, w_out, x, ln_g, ln_b, layer, alpha, *, tm):
    T, K = merged.shape
    D = w_out.shape[-1]
    tm = _tile(T, tm)
    blocks = (_nbytes((tm, K), BF16) + _nbytes((K, D), BF16) + 2 * _nbytes((tm, D), F32)
              + _nbytes((tm, D), BF16))
    return pl.pallas_call(
        functools.partial(_outproj_ln_kernel, alpha=alpha, which=0),
        name=f"outproj_ln_l{layer}",
        grid=(T // tm,),
        in_specs=[
            pl.BlockSpec((tm, K), lambda i: (i, 0)),
            pl.BlockSpec((None, K, D), lambda i: (layer, 0, 0)),
            pl.BlockSpec((tm, D), lambda i: (i, 0)),
            pl.BlockSpec((None, 2, D), lambda i: (layer, 0, 0)),
            pl.BlockSpec((None, 2, D), lambda i: (layer, 0, 0)),
        ],
        out_specs=[pl.BlockSpec((tm, D), lambda i: (i, 0)), pl.BlockSpec((tm, D), lambda i: (i, 0))],
        out_shape=[jax.ShapeDtypeStruct((T, D), F32), jax.ShapeDtypeStruct((T, D), BF16)],
        compiler_params=pltpu.CompilerParams(
            dimension_semantics=("parallel",),
            vmem_limit_bytes=_vmem_limit(blocks, temp_bytes=3 * _nbytes((tm, D), F32))),
    )(merged, w_out, x, ln_g, ln_b)


def _ffn_ln_kernel(xb_ref, wg_ref, wu_ref, wd_ref, x_ref, g_ref, b_ref, of_ref, ob_ref, acc_ref, *, alpha):
    f = pl.program_id(1)

    @pl.when(f == 0)
    def _():
        acc_ref[...] = jnp.zeros(acc_ref.shape, F32)

    xb = xb_ref[...]
    hg = jnp.dot(xb, wg_ref[...], preferred_element_type=F32)
    hu = jnp.dot(xb, wu_ref[...], preferred_element_type=F32)
    act = (hg * _sigmoid(hg) * hu).astype(BF16)
    acc_ref[...] += jnp.dot(act, wd_ref[...], preferred_element_type=F32)

    @pl.when(f == pl.num_programs(1) - 1)
    def _():
        y = _layer_norm_rows(alpha * x_ref[...] + acc_ref[...], g_ref[1:2, :], b_ref[1:2, :])
        of_ref[...] = y
        ob_ref[...] = y.astype(BF16)


def _ffn_ln(xb, x, w_gate_up, w_down, ln_g, ln_b, layer, alpha, *, tm, tf):
    T, D = x.shape
    F = w_down.shape[1]
    tm = _tile(T, tm)
    assert F % tf == 0
    nf = F // tf
    blocks = (_nbytes((tm, D), BF16) + 3 * _nbytes((D, tf), BF16) + 2 * _nbytes((tm, D), F32)
              + _nbytes((tm, D), BF16))
    return pl.pallas_call(
        functools.partial(_ffn_ln_kernel, alpha=alpha),
        name=f"ffn_ln_l{layer}",
        grid=(T // tm, nf),
        in_specs=[
            pl.BlockSpec((tm, D), lambda i, f: (i, 0)),
            pl.BlockSpec((None, D, tf), lambda i, f: (layer, 0, f)),
            pl.BlockSpec((None, D, tf), lambda i, f: (layer, 0, nf + f)),
            pl.BlockSpec((None, tf, D), lambda i, f: (layer, f, 0)),
            pl.BlockSpec((tm, D), lambda i, f: (i, 0)),
            pl.BlockSpec((None, 2, D), lambda i, f: (layer, 0, 0)),
            pl.BlockSpec((None, 2, D), lambda i, f: (layer, 0, 0)),
        ],
        out_specs=[pl.BlockSpec((tm, D), lambda i, f: (i, 0)), pl.BlockSpec((tm, D), lambda i, f: (i, 0))],
        out_shape=[jax.ShapeDtypeStruct((T, D), F32), jax.ShapeDtypeStruct((T, D), BF16)],
        scratch_shapes=[pltpu.VMEM((tm, D), F32)],
        compiler_params=pltpu.CompilerParams(
            dimension_semantics=("parallel", "arbitrary"),
            vmem_limit_bytes=_vmem_limit(blocks, _nbytes((tm, D), F32),
                                         3 * _nbytes((tm, tf), F32) + _nbytes((tm, D), F32))),
    )(xb, w_gate_up, w_gate_up, w_down, x, ln_g, ln_b)


def _rope_tables(seq):
    half = HEAD_DIM // 2
    inv_freq = ROPE_THETA ** (-jnp.arange(half, dtype=F32) * 2.0 / HEAD_DIM)
    ang = jnp.arange(seq, dtype=F32)[:, None] * inv_freq[None, :]
    ang = jnp.concatenate([ang, ang], axis=-1)
    sign = jnp.concatenate([-jnp.ones((half,), F32), jnp.ones((half,), F32)])
    return jnp.cos(ang), jnp.sin(ang) * sign


def kernel(x, w_in, b_merge, diff_lambda, subln_g, conv_w, conv_b, w_rg, b_rg, lru_lambda,
           w_branch, w_out, ln_g, ln_b, w_gate_up, w_down):
    B, S, D = x.shape
    T = B * S
    depth = w_in.shape[0]
    attn_w = w_branch.shape[2]
    d_rnn = conv_w.shape[-1]
    qk_w = (w_in.shape[-1] - attn_w - 2 * d_rnn - 2 * D) // 2
    alpha = (2.0 * depth) ** 0.25

    w_in_b = w_in.astype(BF16)
    wq_t = jnp.swapaxes(w_in[:, :, :qk_w], 1, 2).astype(BF16)
    wv_t = jnp.swapaxes(w_in[:, :, 2 * qk_w:2 * qk_w + attn_w], 1, 2).astype(BF16)
    w_rg_b = (0.5 * w_rg).astype(BF16)
    b_rg_half = 0.5 * b_rg
    w_branch_b = w_branch.astype(BF16)
    w_out_b = w_out.astype(BF16)
    w_gate_up_b = w_gate_up.astype(BF16)
    w_down_b = w_down.astype(BF16)
    subln_g3 = subln_g.reshape(depth, 1, -1)
    conv_b3 = conv_b.reshape(depth, 1, -1)
    lru_lambda3 = lru_lambda.reshape(depth, 1, -1)
    cos_k, sin_k = _rope_tables(S)
    q_scale = HEAD_DIM ** -0.5 * math.log2(math.e)
    rope_q_t = (cos_k.T * q_scale, sin_k.T * q_scale)
    rope_k = (cos_k[None], sin_k[None])

    xf = x.reshape(T, D)
    xb = xf.astype(BF16)
    for l in range(depth):
        lam_init = 0.8 - 0.6 * math.exp(-0.3 * l)
        qt = _project_t(xb, wq_t, l, "proj_qt", tm=1024, tn=1024, rope=rope_q_t, seq=S)
        k = _project(xb, w_in_b, l, qk_w, qk_w, BF16, tm=1024, tn=2048, rope=rope_k, seq=S)
        vt = _project_t(xb, wv_t, l, "proj_vt", tm=1024, tn=2048)
        rest = _project(xb, w_in_b, l, 2 * qk_w + attn_w, 2 * d_rnn + 2 * D, F32, tm=1024, tn=2048)
        o_attn = _diff_attention(qt, k, vt, diff_lambda, subln_g3, l, lam_init,
                                 batch=B, seq=S, tq=512)
        h_rnn = _rglru(rest, conv_w, conv_b3, w_rg_b, b_rg_half, lru_lambda3, l,
                       batch=B, seq=S, width=d_rnn, tt=512)
        merged = _merge(o_attn, h_rnn, w_branch_b, rest, b_merge, l, tm=1024, tn=512)
        xf, xb = _outproj_ln(merged, w_out_b, xf, ln_g, ln_b, l, alpha, tm=512)
        xf, xb = _ffn_ln(xb, xf, w_gate_up_b, w_down_b, ln_g, ln_b, l, alpha, tm=512, tf=FFN_CHUNK)
    return xf.reshape(B, S, D)
```

```python
import functools
import math

import jax
import jax.numpy as jnp
from jax import lax
from jax.experimental import pallas as pl
from jax.experimental.pallas import tpu as pltpu

HEAD_DIM = 128
ROPE_THETA = 10000.0
CONV_WIDTH = 4
LRU_C = 8.0
LN_EPS = 1e-5

LANES = 128
SUBLANES = 8
V7X_VMEM_BYTES = 64 * 1024 * 1024
V7X_VMEM_CEILING = V7X_VMEM_BYTES - 6 * 1024 * 1024

BF16 = jnp.bfloat16
F32 = jnp.float32

FFN_CHUNK = 512


def _vmem_limit(pipelined_bytes, resident_bytes=0, temp_bytes=0):
    need = 2 * pipelined_bytes + resident_bytes + temp_bytes + (4 << 20)
    return int(min(max(need, 16 << 20), V7X_VMEM_CEILING))


def _nbytes(shape, dtype):
    return math.prod(shape) * jnp.dtype(dtype).itemsize


def _sigmoid(x):
    return 0.5 * jnp.tanh(0.5 * x) + 0.5


def _gelu_tanh(x):
    c = math.sqrt(2.0 / math.pi)
    inner = x * (c + (c * 0.044715) * (x * x))
    return x * (0.5 * jnp.tanh(inner) + 0.5)


def _tile(dim, want):
    t = min(dim, want)
    while dim % t:
        t //= 2
    return t


def _proj_kernel(x_ref, w_ref, o_ref):
    acc = jnp.dot(x_ref[...], w_ref[...], preferred_element_type=F32)
    o_ref[...] = acc.astype(o_ref.dtype)


def _proj_rope_kernel(x_ref, w_ref, cos_ref, sin_ref, o_ref):
    acc = jnp.dot(x_ref[...], w_ref[...], preferred_element_type=F32)
    cos = cos_ref[...]
    sin = sin_ref[...]
    for c in range(acc.shape[1] // HEAD_DIM):
        t = acc[:, c * HEAD_DIM:(c + 1) * HEAD_DIM]
        rot = pltpu.roll(t, HEAD_DIM // 2, axis=1)
        o_ref[:, c * HEAD_DIM:(c + 1) * HEAD_DIM] = (t * cos + rot * sin).astype(o_ref.dtype)


def _project(x, w_stack, layer, col0, ncols, out_dtype, *, tm, tn, rope=None, seq=None):
    T, K = x.shape
    tm = _tile(T, tm)
    tn = _tile(ncols, tn)
    assert col0 % tn == 0
    cb0 = col0 // tn
    grid = (T // tm, ncols // tn)
    in_specs = [
        pl.BlockSpec((tm, K), lambda i, j: (i, 0)),
        pl.BlockSpec((None, K, tn), lambda i, j: (layer, 0, j + cb0)),
    ]
    args = [x, w_stack]
    blocks = _nbytes((tm, K), BF16) + _nbytes((K, tn), BF16) + _nbytes((tm, tn), out_dtype)
    if rope is None:
        body = _proj_kernel
    else:
        cos_t, sin_t = rope
        groups = cos_t.shape[0]
        per = (ncols // tn) // groups
        tms = _tile(seq, tm)
        assert tms == tm
        nsb = seq // tm
        tab_spec = pl.BlockSpec((None, tm, HEAD_DIM), lambda i, j: (j // per, i % nsb, 0))
        in_specs += [tab_spec, tab_spec]
        args += [cos_t, sin_t]
        blocks += 2 * _nbytes((tm, HEAD_DIM), F32)
        body = _proj_rope_kernel
    return pl.pallas_call(
        body,
        name=f"proj_c{col0}_l{layer}",
        grid=grid,
        in_specs=in_specs,
        out_specs=pl.BlockSpec((tm, tn), lambda i, j: (i, j)),
        out_shape=jax.ShapeDtypeStruct((T, ncols), out_dtype),
        compiler_params=pltpu.CompilerParams(
            dimension_semantics=("parallel", "arbitrary"),
            vmem_limit_bytes=_vmem_limit(blocks, temp_bytes=2 * _nbytes((tm, tn), F32))),
    )(*args)


_NT = (((1,), (1,)), ((), ()))


def _proj_t_kernel(w_ref, x_ref, o_ref):
    acc = lax.dot_general(w_ref[...], x_ref[...], _NT, preferred_element_type=F32)
    o_ref[...] = acc.astype(o_ref.dtype)


def _proj_t_rope_kernel(w_ref, x_ref, cos_ref, sin_ref, o_ref):
    acc = lax.dot_general(w_ref[...], x_ref[...], _NT, preferred_element_type=F32)
    cos = cos_ref[...]
    sin = sin_ref[...]
    half = HEAD_DIM // 2
    for c in range(acc.shape[0] // HEAD_DIM):
        t = acc[c * HEAD_DIM:(c + 1) * HEAD_DIM, :]
        rot = jnp.concatenate([t[half:], t[:half]], axis=0)
        o_ref[c * HEAD_DIM:(c + 1) * HEAD_DIM, :] = (t * cos + rot * sin).astype(o_ref.dtype)


def _project_t(x, wt_stack, layer, name, *, tm, tn, rope=None, seq=None):
    T, K = x.shape
    N = wt_stack.shape[1]
    tm = _tile(T, tm)
    tn = _tile(N, tn)
    in_specs = [
        pl.BlockSpec((None, tn, K), lambda i, j: (layer, j, 0)),
        pl.BlockSpec((tm, K), lambda i, j: (i, 0)),
    ]
    args = [wt_stack, x]
    blocks = _nbytes((tm, K), BF16) + _nbytes((K, tn), BF16) + _nbytes((tm, tn), BF16)
    if rope is None:
        body = _proj_t_kernel
    else:
        assert seq % tm == 0
        nsb = seq // tm
        tab_spec = pl.BlockSpec((HEAD_DIM, tm), lambda i, j: (0, i % nsb))
        in_specs += [tab_spec, tab_spec]
        args += list(rope)
        blocks += 2 * _nbytes((tm, HEAD_DIM), F32)
        body = _proj_t_rope_kernel
    return pl.pallas_call(
        body,
        name=f"{name}_l{layer}",
        grid=(T // tm, N // tn),
        in_specs=in_specs,
        out_specs=pl.BlockSpec((tn, tm), lambda i, j: (j, i)),
        out_shape=jax.ShapeDtypeStruct((N, T), BF16),
        compiler_params=pltpu.CompilerParams(
            dimension_semantics=("parallel", "arbitrary"),
            vmem_limit_bytes=_vmem_limit(blocks, temp_bytes=2 * _nbytes((tm, tn), F32))),
    )(*args)


def _attn_kernel(lam_ref, g_ref, qt_ref, k_ref, vt_ref, o_ref, m_ref, acc_ref, va_ref,
                 sa_ref, sb_ref, xa_ref, xb_ref, *, tq, nq, lam_init):
    hw = vt_ref.shape[0]
    va_ref[0:hw, :] = vt_ref[...]
    va_ref[hw:, :] = jnp.ones((va_ref.shape[0] - hw, va_ref.shape[1]), BF16)
    lv = lam_ref[...]
    lam = (jnp.exp(jnp.sum(lv[0:1] * lv[1:2], axis=-1, keepdims=True))
           - jnp.exp(jnp.sum(lv[2:3] * lv[3:4], axis=-1, keepdims=True)) + lam_init)
    gain = g_ref[...] * (1.0 - lam_init)

    def scores_into(s_ref, x_ref, qi, kj, diagonal):
        kblk = k_ref[pl.ds(pl.multiple_of(kj * tq, tq), tq), :]
        qcols = pl.ds(pl.multiple_of(qi * tq, tq), tq)
        for c in range(2):
            st = jnp.dot(kblk[:, c * HEAD_DIM:(c + 1) * HEAD_DIM],
                         qt_ref[c * HEAD_DIM:(c + 1) * HEAD_DIM, qcols], preferred_element_type=F32)
            if diagonal:
                key = lax.broadcasted_iota(jnp.int32, st.shape, 0)
                qry = lax.broadcasted_iota(jnp.int32, st.shape, 1)
                st = jnp.where(qry >= key, st, -jnp.inf)
            s_ref[c] = st
            x_ref[c] = jnp.max(st, axis=0, keepdims=True)

    def softmax_pv(s_ref, x_ref, kj):
        vtblk = va_ref[:, pl.ds(pl.multiple_of(kj * tq, tq), tq)]
        for c in range(2):
            m_old = m_ref[c]
            m_new = jnp.maximum(m_old, x_ref[c])
            alpha = jnp.exp2(m_old - m_new)
            pt = jnp.exp2(s_ref[c] - m_new)
            acc_ref[c] = alpha * acc_ref[c] + jnp.dot(vtblk, pt.astype(BF16), preferred_element_type=F32)
            m_ref[c] = m_new

    A = (sa_ref, xa_ref)
    B = (sb_ref, xb_ref)

    def query_block(qi, carry):
        m_ref[...] = jnp.full(m_ref.shape, -jnp.inf, F32)
        acc_ref[...] = jnp.zeros(acc_ref.shape, F32)

        def round_of_two(r, c2):
            scores_into(*B, qi, 2 * r + 1, False)
            softmax_pv(*A, 2 * r)
            scores_into(*A, qi, 2 * r + 2, False)
            softmax_pv(*B, 2 * r + 1)
            return c2

        lax.fori_loop(0, jnp.maximum(qi - 1, 0) // 2, round_of_two, 0)

        @pl.when(qi == 0)
        def _():
            softmax_pv(*A, 0)

        @pl.when(qi % 2 == 1)
        def _():
            scores_into(*B, qi, qi, True)
            softmax_pv(*A, qi - 1)
            softmax_pv(*B, qi)

        @pl.when(jnp.logical_and(qi % 2 == 0, qi > 0))
        def _():
            scores_into(*B, qi, qi - 1, False)
            softmax_pv(*A, qi - 2)
            scores_into(*A, qi, qi, True)
            softmax_pv(*B, qi - 1)
            softmax_pv(*A, qi)

        scores_into(*A, jnp.minimum(qi + 1, nq - 1), 0, False)
        ot = (acc_ref[0, 0:hw, :] * (1.0 / acc_ref[0, hw:hw + 1, :])
              - lam * (acc_ref[1, 0:hw, :] * (1.0 / acc_ref[1, hw:hw + 1, :])))
        ms = jnp.mean(ot * ot, axis=0, keepdims=True)
        ot = ot * lax.rsqrt(ms + LN_EPS)
        o_ref[pl.ds(pl.multiple_of(qi * tq, tq), tq), :] = (ot.T * gain).astype(o_ref.dtype)
        return carry

    scores_into(*A, 0, 0, True)
    lax.fori_loop(0, nq, query_block, 0)


def _diff_attention(qt, k, vt, lam_params, subln_g, layer, lam_init, *, batch, seq, tq):
    T, W = k.shape
    hw = 2 * HEAD_DIM
    heads = W // hw
    tq = _tile(seq, tq)
    nq = seq // tq
    blocks = 4 * _nbytes((seq, hw), BF16)
    scratch = (2 * _nbytes((tq, hw + 16), F32) + 8 * _nbytes((SUBLANES, tq), F32) + 4 * _nbytes((tq, tq), F32)
               + _nbytes((hw + 16, seq), BF16))
    return pl.pallas_call(
        functools.partial(_attn_kernel, tq=tq, nq=nq, lam_init=lam_init),
        name=f"diff_attn_l{layer}",
        grid=(batch, heads),
        in_specs=[
            pl.BlockSpec((None, 4, HEAD_DIM), lambda b, h: (layer, 0, 0)),
            pl.BlockSpec((None, 1, hw), lambda b, h: (layer, 0, 0)),
            pl.BlockSpec((hw, seq), lambda b, h: (h, b)),
            pl.BlockSpec((seq, hw), lambda b, h: (b, h)),
            pl.BlockSpec((hw, seq), lambda b, h: (h, b)),
        ],
        out_specs=pl.BlockSpec((seq, hw), lambda b, h: (b, h)),
        out_shape=jax.ShapeDtypeStruct((T, W), BF16),
        scratch_shapes=[
            pltpu.VMEM((2, 1, tq), F32),
            pltpu.VMEM((2, hw + 16, tq), F32),
            pltpu.VMEM((hw + 16, seq), BF16),
            pltpu.VMEM((2, tq, tq), F32),
            pltpu.VMEM((2, tq, tq), F32),
            pltpu.VMEM((2, 1, tq), F32),
            pltpu.VMEM((2, 1, tq), F32),
        ],
        compiler_params=pltpu.CompilerParams(
            dimension_semantics=("parallel", "parallel"),
            vmem_limit_bytes=_vmem_limit(blocks, scratch, 6 * _nbytes((tq, tq), F32))),
    )(lam_params, subln_g, qt, k, vt)


def _rows_down_one(y, first_row):
    n, C = y.shape
    y3 = y.reshape(n // SUBLANES, SUBLANES, C)
    rolled = pltpu.roll(y3, 1, axis=1)
    lead = jnp.broadcast_to(first_row, (SUBLANES, C))[None]
    prev = jnp.concatenate([lead, rolled[:n // SUBLANES - 1]], axis=0)
    sub = lax.broadcasted_iota(jnp.int32, y3.shape, 1)
    return jnp.where(sub >= 1, rolled, prev).reshape(n, C)


def _segment_end_scan(q, e, h0):
    n, C = q.shape
    ng = n // SUBLANES
    q3 = q.reshape(ng, SUBLANES, C)
    e3 = e.reshape(ng, SUBLANES, C)
    sub = lax.broadcasted_iota(jnp.int32, q3.shape, 1)
    for s in (1, 2, 4):
        keep = sub >= s
        e_new = q3 * pltpu.roll(e3, s, axis=1) + e3
        q_new = q3 * pltpu.roll(q3, s, axis=1)
        e3 = jnp.where(keep, e_new, e3)
        q3 = jnp.where(keep, q_new, q3)
    groups = []
    carry = jnp.broadcast_to(h0, (SUBLANES, C))
    for g in range(ng):
        fg = q3[g] * carry + e3[g]
        groups.append(fg)
        carry = jnp.broadcast_to(fg[SUBLANES - 1:SUBLANES, :], fg.shape)
    return jnp.concatenate(groups, axis=0)


def _rnn_kernel(xr_ref, gr_ref, cw_ref, cb_ref, wrg_ref, brg_ref, lam_ref, o_ref,
                xs, hs, xlast, hcar, *, tt, nblk):
    t = pl.program_id(1)
    C = xr_ref.shape[1]
    bw = C // nblk
    nslab = C // LANES
    nseg = tt // SUBLANES
    P = SUBLANES

    @pl.when(t == 0)
    def _():
        xlast[...] = jnp.zeros(xlast.shape, F32)
        hcar[...] = jnp.zeros(hcar.shape, F32)

    for n in range(nslab):
        xs[n] = xr_ref[:, n * LANES:(n + 1) * LANES]

    def phase(k):
        return jnp.concatenate([xs[n, pl.ds(k, nseg, stride=P), :] for n in range(nslab)], axis=1)

    xp = [phase(k) for k in range(P)]
    prev_rows = xlast[...]
    xlast[...] = xr_ref[tt - P:tt, :]
    xp_down = {k: _rows_down_one(xp[k], prev_rows[k:k + 1]) for k in range(P - CONV_WIDTH + 1, P)}

    cw = cw_ref[...]
    xc = []
    for k in range(P):
        acc = cw[CONV_WIDTH - 1:CONV_WIDTH] * xp[k] + cb_ref[...]
        for d in range(1, CONV_WIDTH):
            src = xp[k - d] if k >= d else xp_down[k - d + P]
            acc = acc + cw[CONV_WIDTH - 1 - d:CONV_WIDTH - d] * src
        xc.append(acc)
    xc_all = jnp.concatenate(xc, axis=0)

    for n in range(nblk):
        cs = slice(n * bw, (n + 1) * bw)
        xcn = xc_all[:, cs]
        xb = xcn.astype(BF16)
        tr = jnp.tanh(jnp.dot(xb, wrg_ref[0, n], preferred_element_type=F32) + brg_ref[0:1, cs])
        ti = jnp.tanh(jnp.dot(xb, wrg_ref[1, n], preferred_element_type=F32) + brg_ref[1:2, cs])
        i = 0.5 * ti + 0.5
        half_rate = (-0.5 * LRU_C * math.log2(math.e)) * jax.nn.softplus(-lam_ref[:, cs])
        a = jnp.exp2(half_rate * tr + half_rate)
        t1 = 1.0 - a * a
        u = jnp.where(t1 > 0.0, t1 * lax.rsqrt(t1), 0.0) * (i * xcn)

        hk = [u[0:nseg]]
        pk = [a[0:nseg]]
        for k in range(1, P):
            ak = a[k * nseg:(k + 1) * nseg]
            hk.append(ak * hk[-1] + u[k * nseg:(k + 1) * nseg])
            pk.append(ak * pk[-1])
        h0 = hcar[0:1, cs]
        f = _segment_end_scan(pk[P - 1], hk[P - 1], h0)
        hcar[:, cs] = jnp.broadcast_to(f[nseg - 1:nseg, :], (SUBLANES, bw))
        g = _rows_down_one(f, h0)
        for k in range(P):
            hk_true = hk[k] + pk[k] * g
            for j in range(bw // LANES):
                hs[n * (bw // LANES) + j, pl.ds(k, nseg, stride=P), :] = hk_true[:, j * LANES:(j + 1) * LANES]

    h = jnp.concatenate([hs[n] for n in range(nslab)], axis=1)
    o_ref[...] = (h * _gelu_tanh(gr_ref[...])).astype(o_ref.dtype)


def _rglru(rest, conv_w, conv_b, w_rg, b_rg, lru_lambda, layer, *, batch, seq, width, tt):
    T = rest.shape[0]
    nblk = w_rg.shape[2]
    tt = _tile(seq, tt)
    nt = seq // tt
    blocks = 2 * _nbytes((tt, width), F32) + _nbytes((tt, width), BF16) + _nbytes(w_rg.shape[1:], BF16)
    scratch = 2 * _nbytes((tt + 2 * SUBLANES, width), F32)
    return pl.pallas_call(
        functools.partial(_rnn_kernel, tt=tt, nblk=nblk),
        name=f"rglru_l{layer}",
        grid=(batch, nt),
        in_specs=[
            pl.BlockSpec((tt, width), lambda b, t: (b * nt + t, 0)),
            pl.BlockSpec((tt, width), lambda b, t: (b * nt + t, 1)),
            pl.BlockSpec((None, CONV_WIDTH, width), lambda b, t: (layer, 0, 0)),
            pl.BlockSpec((None, 1, width), lambda b, t: (layer, 0, 0)),
            pl.BlockSpec((None,) + w_rg.shape[1:], lambda b, t: (layer, 0, 0, 0, 0)),
            pl.BlockSpec((None, 2, width), lambda b, t: (layer, 0, 0)),
            pl.BlockSpec((None, 1, width), lambda b, t: (layer, 0, 0)),
        ],
        out_specs=pl.BlockSpec((tt, width), lambda b, t: (b * nt + t, 0)),
        out_shape=jax.ShapeDtypeStruct((T, width), BF16),
        scratch_shapes=[
            pltpu.VMEM((width // LANES, tt, LANES), F32),
            pltpu.VMEM((width // LANES, tt, LANES), F32),
            pltpu.VMEM((SUBLANES, width), F32),
            pltpu.VMEM((SUBLANES, width), F32),
        ],
        compiler_params=pltpu.CompilerParams(
            dimension_semantics=("parallel", "arbitrary"),
            vmem_limit_bytes=_vmem_limit(blocks, scratch, 6 * _nbytes((tt, width), F32))),
    )(rest, rest, conv_w, conv_b, w_rg, b_rg, lru_lambda)


def _merge_kernel(oa_ref, hr_ref, wa_ref, wr_ref, ga_ref, gr_ref, bm_ref, o_ref):
    ya = jnp.dot(oa_ref[...], wa_ref[...], preferred_element_type=F32)
    yr = jnp.dot(hr_ref[...], wr_ref[...], preferred_element_type=F32)
    ga = _sigmoid(ga_ref[...] + bm_ref[0:1, :])
    gr = _sigmoid(gr_ref[...] + bm_ref[1:2, :])
    o_ref[...] = (ga * ya + gr * yr).astype(o_ref.dtype)


def _merge(o_attn, h_rnn, w_branch, rest, b_merge, layer, *, tm, tn):
    T, K = o_attn.shape
    D = w_branch.shape[-1]
    tm = _tile(T, tm)
    tn = _tile(D, tn)
    nb = D // tn
    gate0 = (rest.shape[1] - 2 * D) // tn
    blocks = (2 * _nbytes((tm, K), BF16) + 2 * _nbytes((K, tn), BF16) + 2 * _nbytes((tm, tn), F32)
              + _nbytes((tm, tn), BF16))
    return pl.pallas_call(
        _merge_kernel,
        name=f"merge_l{layer}",
        grid=(T // tm, nb),
        in_specs=[
            pl.BlockSpec((tm, K), lambda i, j: (i, 0)),
            pl.BlockSpec((tm, K), lambda i, j: (i, 0)),
            pl.BlockSpec((None, None, K, tn), lambda i, j: (layer, 0, 0, j)),
            pl.BlockSpec((None, None, K, tn), lambda i, j: (layer, 1, 0, j)),
            pl.BlockSpec((tm, tn), lambda i, j: (i, gate0 + j)),
            pl.BlockSpec((tm, tn), lambda i, j: (i, gate0 + nb + j)),
            pl.BlockSpec((None, 2, tn), lambda i, j: (layer, 0, j)),
        ],
        out_specs=pl.BlockSpec((tm, tn), lambda i, j: (i, j)),
        out_shape=jax.ShapeDtypeStruct((T, D), BF16),
        compiler_params=pltpu.CompilerParams(
            dimension_semantics=("parallel", "arbitrary"),
            vmem_limit_bytes=_vmem_limit(blocks, temp_bytes=4 * _nbytes((tm, tn), F32))),
    )(o_attn, h_rnn, w_branch, w_branch, rest, rest, b_merge)


def _layer_norm_rows(xf, g, b):
    mu = jnp.mean(xf, axis=-1, keepdims=True)
    d = xf - mu
    var = jnp.mean(d * d, axis=-1, keepdims=True)
    return d * lax.rsqrt(var + LN_EPS) * g + b


def _outproj_ln_kernel(m_ref, w_ref, x_ref, g_ref, b_ref, of_ref, ob_ref, *, alpha, which):
    mix = jnp.dot(m_ref[...], w_ref[...], preferred_element_type=F32)
    y = _layer_norm_rows(alpha * x_ref[...] + mix, g_ref[which:which + 1, :], b_ref[which:which + 1, :])
    of_ref[...] = y
    ob_ref[...] = y.astype(BF16)


def _outproj_ln(merged, w_out, x, ln_g, ln_b, layer, alpha, *, tm):
    T, K = merged.shape
    D = w_out.shape[-1]
    tm = _tile(T, tm)
    blocks = (_nbytes((tm, K), BF16) + _nbytes((K, D), BF16) + 2 * _nbytes((tm, D), F32)
              + _nbytes((tm, D), BF16))
    return pl.pallas_call(
        functools.partial(_outproj_ln_kernel, alpha=alpha, which=0),
        name=f"outproj_ln_l{layer}",
        grid=(T // tm,),
        in_specs=[
            pl.BlockSpec((tm, K), lambda i: (i, 0)),
            pl.BlockSpec((None, K, D), lambda i: (layer, 0, 0)),
            pl.BlockSpec((tm, D), lambda i: (i, 0)),
            pl.BlockSpec((None, 2, D), lambda i: (layer, 0, 0)),
            pl.BlockSpec((None, 2, D), lambda i: (layer, 0, 0)),
        ],
        out_specs=[pl.BlockSpec((tm, D), lambda i: (i, 0)), pl.BlockSpec((tm, D), lambda i: (i, 0))],
        out_shape=[jax.ShapeDtypeStruct((T, D), F32), jax.ShapeDtypeStruct((T, D), BF16)],
        compiler_params=pltpu.CompilerParams(
            dimension_semantics=("parallel",),
            vmem_limit_bytes=_vmem_limit(blocks, temp_bytes=3 * _nbytes((tm, D), F32))),
    )(merged, w_out, x, ln_g, ln_b)


def _ffn_ln_kernel(xb_ref, wg_ref, wu_ref, wd_ref, x_ref, g_ref, b_ref, of_ref, ob_ref, acc_ref, *, alpha):
    f = pl.program_id(1)

    @pl.when(f == 0)
    def _():
        acc_ref[...] = jnp.zeros(acc_ref.shape, F32)

    xb = xb_ref[...]
    hg = jnp.dot(xb, wg_ref[...], preferred_element_type=F32)
    hu = jnp.dot(xb, wu_ref[...], preferred_element_type=F32)
    act = (hg * _sigmoid(hg) * hu).astype(BF16)
    acc_ref[...] += jnp.dot(act, wd_ref[...], preferred_element_type=F32)

    @pl.when(f == pl.num_programs(1) - 1)
    def _():
        y = _layer_norm_rows(alpha * x_ref[...] + acc_ref[...], g_ref[1:2, :], b_ref[1:2, :])
        of_ref[...] = y
        ob_ref[...] = y.astype(BF16)


def _ffn_ln(xb, x, w_gate_up, w_down, ln_g, ln_b, layer, alpha, *, tm, tf):
    T, D = x.shape
    F = w_down.shape[1]
    tm = _tile(T, tm)
    assert F % tf == 0
    nf = F // tf
    blocks = (_nbytes((tm, D), BF16) + 3 * _nbytes((D, tf), BF16) + 2 * _nbytes((tm, D), F32)
              + _nbytes((tm, D), BF16))
    return pl.pallas_call(
        functools.partial(_ffn_ln_kernel, alpha=alpha),
        name=f"ffn_ln_l{layer}",
        grid=(T // tm, nf),
        in_specs=[
            pl.BlockSpec((tm, D), lambda i, f: (i, 0)),
            pl.BlockSpec((None, D, tf), lambda i, f: (layer, 0, f)),
            pl.BlockSpec((None, D, tf), lambda i, f: (layer, 0, nf + f)),
            pl.BlockSpec((None, tf, D), lambda i, f: (layer, f, 0)),
            pl.BlockSpec((tm, D), lambda i, f: (i, 0)),
            pl.BlockSpec((None, 2, D), lambda i, f: (layer, 0, 0)),
            pl.BlockSpec((None, 2, D), lambda i, f: (layer, 0, 0)),
        ],
        out_specs=[pl.BlockSpec((tm, D), lambda i, f: (i, 0)), pl.BlockSpec((tm, D), lambda i, f: (i, 0))],
        out_shape=[jax.ShapeDtypeStruct((T, D), F32), jax.ShapeDtypeStruct((T, D), BF16)],
        scratch_shapes=[pltpu.VMEM((tm, D), F32)],
        compiler_params=pltpu.CompilerParams(
            dimension_semantics=("parallel", "arbitrary"),
            vmem_limit_bytes=_vmem_limit(blocks, _nbytes((tm, D), F32),
                                         3 * _nbytes((tm, tf), F32) + _nbytes((tm, D), F32))),
    )(xb, w_gate_up, w_gate_up, w_down, x, ln_g, ln_b)


def _rope_tables(seq):
    half = HEAD_DIM // 2
    inv_freq = ROPE_THETA ** (-jnp.arange(half, dtype=F32) * 2.0 / HEAD_DIM)
    ang = jnp.arange(seq, dtype=F32)[:, None] * inv_freq[None, :]
    ang = jnp.concatenate([ang, ang], axis=-1)
    sign = jnp.concatenate([-jnp.ones((half,), F32), jnp.ones((half,), F32)])
    return jnp.cos(ang), jnp.sin(ang) * sign


def kernel(x, w_in, b_merge, diff_lambda, subln_g, conv_w, conv_b, w_rg, b_rg, lru_lambda,
           w_branch, w_out, ln_g, ln_b, w_gate_up, w_down):
    B, S, D = x.shape
    T = B * S
    depth = w_in.shape[0]
    attn_w = w_branch.shape[2]
    d_rnn = conv_w.shape[-1]
    qk_w = (w_in.shape[-1] - attn_w - 2 * d_rnn - 2 * D) // 2
    alpha = (2.0 * depth) ** 0.25

    w_in_b = w_in.astype(BF16)
    wq_t = jnp.swapaxes(w_in[:, :, :qk_w], 1, 2).astype(BF16)
    wv_t = jnp.swapaxes(w_in[:, :, 2 * qk_w:2 * qk_w + attn_w], 1, 2).astype(BF16)
    w_rg_b = (0.5 * w_rg).astype(BF16)
    b_rg_half = 0.5 * b_rg
    w_branch_b = w_branch.astype(BF16)
    w_out_b = w_out.astype(BF16)
    w_gate_up_b = w_gate_up.astype(BF16)
    w_down_b = w_down.astype(BF16)
    subln_g3 = subln_g.reshape(depth, 1, -1)
    conv_b3 = conv_b.reshape(depth, 1, -1)
    lru_lambda3 = lru_lambda.reshape(depth, 1, -1)
    cos_k, sin_k = _rope_tables(S)
    q_scale = HEAD_DIM ** -0.5 * math.log2(math.e)
    rope_q_t = (cos_k.T * q_scale, sin_k.T * q_scale)
    rope_k = (cos_k[None], sin_k[None])

    xf = x.reshape(T, D)
    xb = xf.astype(BF16)
    for l in range(depth):
        lam_init = 0.8 - 0.6 * math.exp(-0.3 * l)
        qt = _project_t(xb, wq_t, l, "proj_qt", tm=1024, tn=1024, rope=rope_q_t, seq=S)
        k = _project(xb, w_in_b, l, qk_w, qk_w, BF16, tm=1024, tn=2048, rope=rope_k, seq=S)
        vt = _project_t(xb, wv_t, l, "proj_vt", tm=1024, tn=2048)
        rest = _project(xb, w_in_b, l, 2 * qk_w + attn_w, 2 * d_rnn + 2 * D, F32, tm=1024, tn=2048)
        o_attn = _diff_attention(qt, k, vt, diff_lambda, subln_g3, l, lam_init,
                                 batch=B, seq=S, tq=512)
        h_rnn = _rglru(rest, conv_w, conv_b3, w_rg_b, b_rg_half, lru_lambda3, l,
                       batch=B, seq=S, width=d_rnn, tt=512)
        merged = _merge(o_attn, h_rnn, w_branch_b, rest, b_merge, l, tm=1024, tn=512)
        xf, xb = _outproj_ln(merged, w_out_b, xf, ln_g, ln_b, l, alpha, tm=512)
        xf, xb = _ffn_ln(xb, xf, w_gate_up_b, w_down_b, ln_g, ln_b, l, alpha, tm=512, tf=FFN_CHUNK)
    return xf.reshape(B, S, D)
```

```python
import functools
import math

import jax
import jax.numpy as jnp
from jax import lax
from jax.experimental import pallas as pl
from jax.experimental.pallas import tpu as pltpu

HEAD_DIM = 128
ROPE_THETA = 10000.0
CONV_WIDTH = 4
LRU_C = 8.0
LN_EPS = 1e-5

LANES = 128
SUBLANES = 8
V7X_VMEM_BYTES = 64 * 1024 * 1024
V7X_VMEM_CEILING = V7X_VMEM_BYTES - 6 * 1024 * 1024

BF16 = jnp.bfloat16
F32 = jnp.float32

FFN_CHUNK = 512


def _vmem_limit(pipelined_bytes, resident_bytes=0, temp_bytes=0):
    need = 2 * pipelined_bytes + resident_bytes + temp_bytes + (4 << 20)
    return int(min(max(need, 16 << 20), V7X_VMEM_CEILING))


def _nbytes(shape, dtype):
    return math.prod(shape) * jnp.dtype(dtype).itemsize


def _sigmoid(x):
    return 0.5 * jnp.tanh(0.5 * x) + 0.5


def _gelu_tanh(x):
    c = math.sqrt(2.0 / math.pi)
    inner = x * (c + (c * 0.044715) * (x * x))
    return x * (0.5 * jnp.tanh(inner) + 0.5)


def _tile(dim, want):
    t = min(dim, want)
    while dim % t:
        t //= 2
    return t


def _proj_kernel(x_ref, w_ref, o_ref):
    acc = jnp.dot(x_ref[...], w_ref[...], preferred_element_type=F32)
    o_ref[...] = acc.astype(o_ref.dtype)


def _proj_rope_kernel(x_ref, w_ref, cos_ref, sin_ref, o_ref):
    acc = jnp.dot(x_ref[...], w_ref[...], preferred_element_type=F32)
    cos = cos_ref[...]
    sin = sin_ref[...]
    for c in range(acc.shape[1] // HEAD_DIM):
        t = acc[:, c * HEAD_DIM:(c + 1) * HEAD_DIM]
        rot = pltpu.roll(t, HEAD_DIM // 2, axis=1)
        o_ref[:, c * HEAD_DIM:(c + 1) * HEAD_DIM] = (t * cos + rot * sin).astype(o_ref.dtype)


def _project(x, w_stack, layer, col0, ncols, out_dtype, *, tm, tn, rope=None, seq=None):
    T, K = x.shape
    tm = _tile(T, tm)
    tn = _tile(ncols, tn)
    assert col0 % tn == 0
    cb0 = col0 // tn
    grid = (T // tm, ncols // tn)
    in_specs = [
        pl.BlockSpec((tm, K), lambda i, j: (i, 0)),
        pl.BlockSpec((None, K, tn), lambda i, j: (layer, 0, j + cb0)),
    ]
    args = [x, w_stack]
    blocks = _nbytes((tm, K), BF16) + _nbytes((K, tn), BF16) + _nbytes((tm, tn), out_dtype)
    if rope is None:
        body = _proj_kernel
    else:
        cos_t, sin_t = rope
        groups = cos_t.shape[0]
        per = (ncols // tn) // groups
        tms = _tile(seq, tm)
        assert tms == tm
        nsb = seq // tm
        tab_spec = pl.BlockSpec((None, tm, HEAD_DIM), lambda i, j: (j // per, i % nsb, 0))
        in_specs += [tab_spec, tab_spec]
        args += [cos_t, sin_t]
        blocks += 2 * _nbytes((tm, HEAD_DIM), F32)
        body = _proj_rope_kernel
    return pl.pallas_call(
        body,
        name=f"proj_c{col0}_l{layer}",
        grid=grid,
        in_specs=in_specs,
        out_specs=pl.BlockSpec((tm, tn), lambda i, j: (i, j)),
        out_shape=jax.ShapeDtypeStruct((T, ncols), out_dtype),
        compiler_params=pltpu.CompilerParams(
            dimension_semantics=("parallel", "arbitrary"),
            vmem_limit_bytes=_vmem_limit(blocks, temp_bytes=2 * _nbytes((tm, tn), F32))),
    )(*args)


_NT = (((1,), (1,)), ((), ()))


def _proj_t_kernel(w_ref, x_ref, o_ref):
    acc = lax.dot_general(w_ref[...], x_ref[...], _NT, preferred_element_type=F32)
    o_ref[...] = acc.astype(o_ref.dtype)


def _proj_t_rope_kernel(w_ref, x_ref, cos_ref, sin_ref, o_ref):
    acc = lax.dot_general(w_ref[...], x_ref[...], _NT, preferred_element_type=F32)
    cos = cos_ref[...]
    sin = sin_ref[...]
    half = HEAD_DIM // 2
    for c in range(acc.shape[0] // HEAD_DIM):
        t = acc[c * HEAD_DIM:(c + 1) * HEAD_DIM, :]
        rot = jnp.concatenate([t[half:], t[:half]], axis=0)
        o_ref[c * HEAD_DIM:(c + 1) * HEAD_DIM, :] = (t * cos + rot * sin).astype(o_ref.dtype)


def _project_t(x, wt_stack, layer, name, *, tm, tn, rope=None, seq=None):
    T, K = x.shape
    N = wt_stack.shape[1]
    tm = _tile(T, tm)
    tn = _tile(N, tn)
    in_specs = [
        pl.BlockSpec((None, tn, K), lambda i, j: (layer, j, 0)),
        pl.BlockSpec((tm, K), lambda i, j: (i, 0)),
    ]
    args = [wt_stack, x]
    blocks = _nbytes((tm, K), BF16) + _nbytes((K, tn), BF16) + _nbytes((tm, tn), BF16)
    if rope is None:
        body = _proj_t_kernel
    else:
        assert seq % tm == 0
        nsb = seq // tm
        tab_spec = pl.BlockSpec((HEAD_DIM, tm), lambda i, j: (0, i % nsb))
        in_specs += [tab_spec, tab_spec]
        args += list(rope)
        blocks += 2 * _nbytes((tm, HEAD_DIM), F32)
        body = _proj_t_rope_kernel
    return pl.pallas_call(
        body,
        name=f"{name}_l{layer}",
        grid=(T // tm, N // tn),
        in_specs=in_specs,
        out_specs=pl.BlockSpec((tn, tm), lambda i, j: (j, i)),
        out_shape=jax.ShapeDtypeStruct((N, T), BF16),
        compiler_params=pltpu.CompilerParams(
            dimension_semantics=("parallel", "arbitrary"),
            vmem_limit_bytes=_vmem_limit(blocks, temp_bytes=2 * _nbytes((tm, tn), F32))),
    )(*args)


def _attn_kernel(lam_ref, g_ref, qt_ref, k_ref, vt_ref, o_ref, acc_ref, va_ref,
                 sa_ref, sb_ref, m_ref, xa_ref, xb_ref, *, tq, nq, lam_init):
    hw = vt_ref.shape[0]
    va_ref[0:hw, :] = vt_ref[...]
    va_ref[hw:, :] = jnp.ones((va_ref.shape[0] - hw, va_ref.shape[1]), BF16)
    lv = lam_ref[...]
    lam = (jnp.exp(jnp.sum(lv[0:1] * lv[1:2], axis=-1, keepdims=True))
           - jnp.exp(jnp.sum(lv[2:3] * lv[3:4], axis=-1, keepdims=True)) + lam_init)
    gain = g_ref[...] * (1.0 - lam_init)

    def scores_into(s_ref, x_ref, qi, kj, diagonal):
        kblk = k_ref[pl.ds(pl.multiple_of(kj * tq, tq), tq), :]
        qcols = pl.ds(pl.multiple_of(qi * tq, tq), tq)
        for c in range(2):
            st = jnp.dot(kblk[:, c * HEAD_DIM:(c + 1) * HEAD_DIM],
                         qt_ref[c * HEAD_DIM:(c + 1) * HEAD_DIM, qcols], preferred_element_type=F32)
            if diagonal:
                key = lax.broadcasted_iota(jnp.int32, st.shape, 0)
                qry = lax.broadcasted_iota(jnp.int32, st.shape, 1)
                st = jnp.where(qry >= key, st, -jnp.inf)
            s_ref[c] = st
            x_ref[c] = jnp.max(st, axis=0, keepdims=True)

    def softmax_pv(s_ref, x_ref, kj):
        vtblk = va_ref[:, pl.ds(pl.multiple_of(kj * tq, tq), tq)]
        for c in range(2):
            m_old = m_ref[c]
            m_new = jnp.maximum(m_old, x_ref[c])
            alpha = jnp.exp2(m_old - m_new)
            pt = jnp.exp2(s_ref[c] - m_new)
            acc_ref[c] = alpha * acc_ref[c] + jnp.dot(vtblk, pt.astype(BF16), preferred_element_type=F32)
            m_ref[c] = m_new

    A = (sa_ref, xa_ref)
    B = (sb_ref, xb_ref)

    def query_block(qi, carry):
        m_ref[...] = jnp.full(m_ref.shape, -jnp.inf, F32)
        acc_ref[...] = jnp.zeros(acc_ref.shape, F32)

        def round_of_two(r, c2):
            scores_into(*B, qi, 2 * r + 1, False)
            softmax_pv(*A, 2 * r)
            scores_into(*A, qi, 2 * r + 2, False)
            softmax_pv(*B, 2 * r + 1)
            return c2

        lax.fori_loop(0, jnp.maximum(qi - 1, 0) // 2, round_of_two, 0)

        @pl.when(qi == 0)
        def _():
            softmax_pv(*A, 0)

        @pl.when(qi % 2 == 1)
        def _():
            scores_into(*B, qi, qi, True)
            softmax_pv(*A, qi - 1)
            softmax_pv(*B, qi)

        @pl.when(jnp.logical_and(qi % 2 == 0, qi > 0))
        def _():
            scores_into(*B, qi, qi - 1, False)
            softmax_pv(*A, qi - 2)
            scores_into(*A, qi, qi, True)
            softmax_pv(*B, qi - 1)
            softmax_pv(*A, qi)

        scores_into(*A, jnp.minimum(qi + 1, nq - 1), 0, False)
        ot = (acc_ref[0, 0:hw, :] * (1.0 / acc_ref[0, hw:hw + 1, :])
              - lam * (acc_ref[1, 0:hw, :] * (1.0 / acc_ref[1, hw:hw + 1, :])))
        ms = jnp.mean(ot * ot, axis=0, keepdims=True)
        ot = ot * lax.rsqrt(ms + LN_EPS)
        o_ref[pl.ds(pl.multiple_of(qi * tq, tq), tq), :] = (ot.T * gain).astype(o_ref.dtype)
        return carry

    scores_into(*A, 0, 0, True)
    lax.fori_loop(0, nq, query_block, 0)


def _diff_attention(qt, k, vt, lam_params, subln_g, layer, lam_init, *, batch, seq, tq):
    T, W = k.shape
    hw = 2 * HEAD_DIM
    heads = W // hw
    tq = _tile(seq, tq)
    nq = seq // tq
    blocks = 4 * _nbytes((seq, hw), BF16)
    scratch = (2 * _nbytes((tq, hw + 16), F32) + 8 * _nbytes((SUBLANES, tq), F32) + 4 * _nbytes((tq, tq), F32)
               + _nbytes((hw + 16, seq), BF16))
    return pl.pallas_call(
        functools.partial(_attn_kernel, tq=tq, nq=nq, lam_init=lam_init),
        name=f"diff_attn_l{layer}",
        grid=(batch, heads),
        in_specs=[
            pl.BlockSpec((None, 4, HEAD_DIM), lambda b, h: (layer, 0, 0)),
            pl.BlockSpec((None, 1, hw), lambda b, h: (layer, 0, 0)),
            pl.BlockSpec((hw, seq), lambda b, h: (h, b)),
            pl.BlockSpec((seq, hw), lambda b, h: (b, h)),
            pl.BlockSpec((hw, seq), lambda b, h: (h, b)),
        ],
        out_specs=pl.BlockSpec((seq, hw), lambda b, h: (b, h)),
        out_shape=jax.ShapeDtypeStruct((T, W), BF16),
        scratch_shapes=[
            pltpu.VMEM((2, hw + 16, tq), F32),
            pltpu.VMEM((hw + 16, seq), BF16),
            pltpu.VMEM((2, tq, tq), F32),
            pltpu.VMEM((2, tq, tq), F32),
            pltpu.VMEM((2, 1, tq), F32),
            pltpu.VMEM((2, 1, tq), F32),
            pltpu.VMEM((2, 1, tq), F32),
        ],
        compiler_params=pltpu.CompilerParams(
            dimension_semantics=("parallel", "parallel"),
            vmem_limit_bytes=_vmem_limit(blocks, scratch, 6 * _nbytes((tq, tq), F32))),
    )(lam_params, subln_g, qt, k, vt)


def _rows_down_one(y, first_row):
    n, C = y.shape
    y3 = y.reshape(n // SUBLANES, SUBLANES, C)
    rolled = pltpu.roll(y3, 1, axis=1)
    lead = jnp.broadcast_to(first_row, (SUBLANES, C))[None]
    prev = jnp.concatenate([lead, rolled[:n // SUBLANES - 1]], axis=0)
    sub = lax.broadcasted_iota(jnp.int32, y3.shape, 1)
    return jnp.where(sub >= 1, rolled, prev).reshape(n, C)


def _segment_end_scan(q, e, h0):
    n, C = q.shape
    ng = n // SUBLANES
    q3 = q.reshape(ng, SUBLANES, C)
    e3 = e.reshape(ng, SUBLANES, C)
    sub = lax.broadcasted_iota(jnp.int32, q3.shape, 1)
    for s in (1, 2, 4):
        keep = sub >= s
        e_new = q3 * pltpu.roll(e3, s, axis=1) + e3
        q_new = q3 * pltpu.roll(q3, s, axis=1)
        e3 = jnp.where(keep, e_new, e3)
        q3 = jnp.where(keep, q_new, q3)
    groups = []
    carry = jnp.broadcast_to(h0, (SUBLANES, C))
    for g in range(ng):
        fg = q3[g] * carry + e3[g]
        groups.append(fg)
        carry = jnp.broadcast_to(fg[SUBLANES - 1:SUBLANES, :], fg.shape)
    return jnp.concatenate(groups, axis=0)


def _rnn_kernel(xr_ref, gr_ref, cw_ref, cb_ref, wrg_ref, brg_ref, lam_ref, o_ref,
                xs, hs, xlast, hcar, *, tt, nblk):
    t = pl.program_id(1)
    C = xr_ref.shape[1]
    bw = C // nblk
    nslab = C // LANES
    nseg = tt // SUBLANES
    P = SUBLANES

    @pl.when(t == 0)
    def _():
        xlast[...] = jnp.zeros(xlast.shape, F32)
        hcar[...] = jnp.zeros(hcar.shape, F32)

    for n in range(nslab):
        xs[n] = xr_ref[:, n * LANES:(n + 1) * LANES]

    def phase(k):
        return jnp.concatenate([xs[n, pl.ds(k, nseg, stride=P), :] for n in range(nslab)], axis=1)

    xp = [phase(k) for k in range(P)]
    prev_rows = xlast[...]
    xlast[...] = xr_ref[tt - P:tt, :]
    xp_down = {k: _rows_down_one(xp[k], prev_rows[k:k + 1]) for k in range(P - CONV_WIDTH + 1, P)}

    cw = cw_ref[...]
    xc = []
    for k in range(P):
        acc = cw[CONV_WIDTH - 1:CONV_WIDTH] * xp[k] + cb_ref[...]
        for d in range(1, CONV_WIDTH):
            src = xp[k - d] if k >= d else xp_down[k - d + P]
            acc = acc + cw[CONV_WIDTH - 1 - d:CONV_WIDTH - d] * src
        xc.append(acc)
    xc_all = jnp.concatenate(xc, axis=0)

    for n in range(nblk):
        cs = slice(n * bw, (n + 1) * bw)
        xcn = xc_all[:, cs]
        xb = xcn.astype(BF16)
        tr = jnp.tanh(jnp.dot(xb, wrg_ref[0, n], preferred_element_type=F32) + brg_ref[0:1, cs])
        ti = jnp.tanh(jnp.dot(xb, wrg_ref[1, n], preferred_element_type=F32) + brg_ref[1:2, cs])
        i = 0.5 * ti + 0.5
        half_rate = (-0.5 * LRU_C * math.log2(math.e)) * jax.nn.softplus(-lam_ref[:, cs])
        a = jnp.exp2(half_rate * tr + half_rate)
        t1 = 1.0 - a * a
        u = jnp.where(t1 > 0.0, t1 * lax.rsqrt(t1), 0.0) * (i * xcn)

        hk = [u[0:nseg]]
        pk = [a[0:nseg]]
        for k in range(1, P):
            ak = a[k * nseg:(k + 1) * nseg]
            hk.append(ak * hk[-1] + u[k * nseg:(k + 1) * nseg])
            pk.append(ak * pk[-1])
        h0 = hcar[0:1, cs]
        f = _segment_end_scan(pk[P - 1], hk[P - 1], h0)
        hcar[:, cs] = jnp.broadcast_to(f[nseg - 1:nseg, :], (SUBLANES, bw))
        g = _rows_down_one(f, h0)
        for k in range(P):
            hk_true = hk[k] + pk[k] * g
            for j in range(bw // LANES):
                hs[n * (bw // LANES) + j, pl.ds(k, nseg, stride=P), :] = hk_true[:, j * LANES:(j + 1) * LANES]

    h = jnp.concatenate([hs[n] for n in range(nslab)], axis=1)
    o_ref[...] = (h * _gelu_tanh(gr_ref[...])).astype(o_ref.dtype)


def _rglru(rest, conv_w, conv_b, w_rg, b_rg, lru_lambda, layer, *, batch, seq, width, tt):
    T = rest.shape[0]
    nblk = w_rg.shape[2]
    tt = _tile(seq, tt)
    nt = seq // tt
    blocks = 2 * _nbytes((tt, width), F32) + _nbytes((tt, width), BF16) + _nbytes(w_rg.shape[1:], BF16)
    scratch = 2 * _nbytes((tt + 2 * SUBLANES, width), F32)
    return pl.pallas_call(
        functools.partial(_rnn_kernel, tt=tt, nblk=nblk),
        name=f"rglru_l{layer}",
        grid=(batch, nt),
        in_specs=[
            pl.BlockSpec((tt, width), lambda b, t: (b * nt + t, 0)),
            pl.BlockSpec((tt, width), lambda b, t: (b * nt + t, 1)),
            pl.BlockSpec((None, CONV_WIDTH, width), lambda b, t: (layer, 0, 0)),
            pl.BlockSpec((None, 1, width), lambda b, t: (layer, 0, 0)),
            pl.BlockSpec((None,) + w_rg.shape[1:], lambda b, t: (layer, 0, 0, 0, 0)),
            pl.BlockSpec((None, 2, width), lambda b, t: (layer, 0, 0)),
            pl.BlockSpec((None, 1, width), lambda b, t: (layer, 0, 0)),
        ],
        out_specs=pl.BlockSpec((tt, width), lambda b, t: (b * nt + t, 0)),
        out_shape=jax.ShapeDtypeStruct((T, width), BF16),
        scratch_shapes=[
            pltpu.VMEM((width // LANES, tt, LANES), F32),
            pltpu.VMEM((width // LANES, tt, LANES), F32),
            pltpu.VMEM((SUBLANES, width), F32),
            pltpu.VMEM((SUBLANES, width), F32),
        ],
        compiler_params=pltpu.CompilerParams(
            dimension_semantics=("parallel", "arbitrary"),
            vmem_limit_bytes=_vmem_limit(blocks, scratch, 6 * _nbytes((tt, width), F32))),
    )(rest, rest, conv_w, conv_b, w_rg, b_rg, lru_lambda)


def _merge_kernel(oa_ref, hr_ref, wa_ref, wr_ref, ga_ref, gr_ref, bm_ref, o_ref):
    ya = jnp.dot(oa_ref[...], wa_ref[...], preferred_element_type=F32)
    yr = jnp.dot(hr_ref[...], wr_ref[...], preferred_element_type=F32)
    ga = _sigmoid(ga_ref[...] + bm_ref[0:1, :])
    gr = _sigmoid(gr_ref[...] + bm_ref[1:2, :])
    o_ref[...] = (ga * ya + gr * yr).astype(o_ref.dtype)


def _merge(o_attn, h_rnn, w_branch, rest, b_merge, layer, *, tm, tn):
    T, K = o_attn.shape
    D = w_branch.shape[-1]
    tm = _tile(T, tm)
    tn = _tile(D, tn)
    nb = D // tn
    gate0 = (rest.shape[1] - 2 * D) // tn
    blocks = (2 * _nbytes((tm, K), BF16) + 2 * _nbytes((K, tn), BF16) + 2 * _nbytes((tm, tn), F32)
              + _nbytes((tm, tn), BF16))
    return pl.pallas_call(
        _merge_kernel,
        name=f"merge_l{layer}",
        grid=(T // tm, nb),
        in_specs=[
            pl.BlockSpec((tm, K), lambda i, j: (i, 0)),
            pl.BlockSpec((tm, K), lambda i, j: (i, 0)),
            pl.BlockSpec((None, None, K, tn), lambda i, j: (layer, 0, 0, j)),
            pl.BlockSpec((None, None, K, tn), lambda i, j: (layer, 1, 0, j)),
            pl.BlockSpec((tm, tn), lambda i, j: (i, gate0 + j)),
            pl.BlockSpec((tm, tn), lambda i, j: (i, gate0 + nb + j)),
            pl.BlockSpec((None, 2, tn), lambda i, j: (layer, 0, j)),
        ],
        out_specs=pl.BlockSpec((tm, tn), lambda i, j: (i, j)),
        out_shape=jax.ShapeDtypeStruct((T, D), BF16),
        compiler_params=pltpu.CompilerParams(
            dimension_semantics=("parallel", "arbitrary"),
            vmem_limit_bytes=_vmem_limit(blocks, temp_bytes=4 * _nbytes((tm, tn), F32))),
    )(o_attn, h_rnn, w_branch, w_branch, rest, rest, b_merge)


def _layer_norm_rows(xf, g, b):
    mu = jnp.mean(xf, axis=-1, keepdims=True)
    d = xf - mu
    var = jnp.mean(d * d, axis=-1, keepdims=True)
    return d * lax.rsqrt(var + LN_EPS) * g + b


def _outproj_ln_kernel(m_ref, w_ref, x_ref, g_ref, b_ref, of_ref, ob_ref, *, alpha, which):
    mix = jnp.dot(m_ref[...], w_ref[...], preferred_element_type=F32)
    y = _layer_norm_rows(alpha * x_ref[...] + mix, g_ref[which:which + 1, :], b_ref[which:which + 1, :])
    of_ref[...] = y
    ob_ref[...] = y.astype(BF16)


def _outproj_ln(merged, w_out, x, ln_g, ln_b, layer, alpha, *, tm):
    T, K = merged.shape
    D = w_out.shape[-1]
    tm = _tile(T, tm)
    blocks = (_nbytes((tm, K), BF16) + _nbytes((K, D), BF16) + 2 * _nbytes((tm, D), F32)
              + _nbytes((tm, D), BF16))
    return pl.pallas_call(
        functools.partial(_outproj_ln_kernel, alpha=alpha, which=0),
        name=f"outproj_ln_l{layer}",
        grid=(T // tm,),
        in_specs=[
            pl.BlockSpec((tm, K), lambda i: (i, 0)),
            pl.BlockSpec((None, K, D), lambda i: (layer, 0, 0)),
            pl.BlockSpec((tm, D), lambda i: (i, 0)),
            pl.BlockSpec((None, 2, D), lambda i: (layer, 0, 0)),
            pl.BlockSpec((None, 2, D), lambda i: (layer, 0, 0)),
        ],
        out_specs=[pl.BlockSpec((tm, D), lambda i: (i, 0)), pl.BlockSpec((tm, D), lambda i: (i, 0))],
        out_shape=[jax.ShapeDtypeStruct((T, D), F32), jax.ShapeDtypeStruct((T, D), BF16)],
        compiler_params=pltpu.CompilerParams(
            dimension_semantics=("parallel",),
            vmem_limit_bytes=_vmem_limit(blocks, temp_bytes=3 * _nbytes((tm, D), F32))),
    )(merged, w_out, x, ln_g, ln_b)


def _ffn_ln_kernel(xb_ref, wg_ref, wu_ref, wd_ref, x_ref, g_ref, b_ref, of_ref, ob_ref, acc_ref, *, alpha):
    f = pl.program_id(1)

    @pl.when(f == 0)
    def _():
        acc_ref[...] = jnp.zeros(acc_ref.shape, F32)

    xb = xb_ref[...]
    hg = jnp.dot(xb, wg_ref[...], preferred_element_type=F32)
    hu = jnp.dot(xb, wu_ref[...], preferred_element_type=F32)
    act = (hg * _sigmoid(hg) * hu).astype(BF16)
    acc_ref[...] += jnp.dot(act, wd_ref[...], preferred_element_type=F32)

    @pl.when(f == pl.num_programs(1) - 1)
    def _():
        y = _layer_norm_rows(alpha * x_ref[...] + acc_ref[...], g_ref[1:2, :], b_ref[1:2, :])
        of_ref[...] = y
        ob_ref[...] = y.astype(BF16)


def _ffn_ln(xb, x, w_gate_up, w_down, ln_g, ln_b, layer, alpha, *, tm, tf):
    T, D = x.shape
    F = w_down.shape[1]
    tm = _tile(T, tm)
    assert F % tf == 0
    nf = F // tf
    blocks = (_nbytes((tm, D), BF16) + 3 * _nbytes((D, tf), BF16) + 2 * _nbytes((tm, D), F32)
              + _nbytes((tm, D), BF16))
    return pl.pallas_call(
        functools.partial(_ffn_ln_kernel, alpha=alpha),
        name=f"ffn_ln_l{layer}",
        grid=(T // tm, nf),
        in_specs=[
            pl.BlockSpec((tm, D), lambda i, f: (i, 0)),
            pl.BlockSpec((None, D, tf), lambda i, f: (layer, 0, f)),
            pl.BlockSpec((None, D, tf), lambda i, f: (layer, 0, nf + f)),
            pl.BlockSpec((None, tf, D), lambda i, f: (layer, f, 0)),
            pl.BlockSpec((tm, D), lambda i, f: (i, 0)),
            pl.BlockSpec((None, 2, D), lambda i, f: (layer, 0, 0)),
            pl.BlockSpec((None, 2, D), lambda i, f: (layer, 0, 0)),
        ],
        out_specs=[pl.BlockSpec((tm, D), lambda i, f: (i, 0)), pl.BlockSpec((tm, D), lambda i, f: (i, 0))],
        out_shape=[jax.ShapeDtypeStruct((T, D), F32), jax.ShapeDtypeStruct((T, D), BF16)],
        scratch_shapes=[pltpu.VMEM((tm, D), F32)],
        compiler_params=pltpu.CompilerParams(
            dimension_semantics=("parallel", "arbitrary"),
            vmem_limit_bytes=_vmem_limit(blocks, _nbytes((tm, D), F32),
                                         3 * _nbytes((tm, tf), F32) + _nbytes((tm, D), F32))),
    )(xb, w_gate_up, w_gate_up, w_down, x, ln_g, ln_b)


def _rope_tables(seq):
    half = HEAD_DIM // 2
    inv_freq = ROPE_THETA ** (-jnp.arange(half, dtype=F32) * 2.0 / HEAD_DIM)
    ang = jnp.arange(seq, dtype=F32)[:, None] * inv_freq[None, :]
    ang = jnp.concatenate([ang, ang], axis=-1)
    sign = jnp.concatenate([-jnp.ones((half,), F32), jnp.ones((half,), F32)])
    return jnp.cos(ang), jnp.sin(ang) * sign


def kernel(x, w_in, b_merge, diff_lambda, subln_g, conv_w, conv_b, w_rg, b_rg, lru_lambda,
           w_branch, w_out, ln_g, ln_b, w_gate_up, w_down):
    B, S, D = x.shape
    T = B * S
    depth = w_in.shape[0]
    attn_w = w_branch.shape[2]
    d_rnn = conv_w.shape[-1]
    qk_w = (w_in.shape[-1] - attn_w - 2 * d_rnn - 2 * D) // 2
    alpha = (2.0 * depth) ** 0.25

    w_in_b = w_in.astype(BF16)
    wq_t = jnp.swapaxes(w_in[:, :, :qk_w], 1, 2).astype(BF16)
    wv_t = jnp.swapaxes(w_in[:, :, 2 * qk_w:2 * qk_w + attn_w], 1, 2).astype(BF16)
    w_rg_b = (0.5 * w_rg).astype(BF16)
    b_rg_half = 0.5 * b_rg
    w_branch_b = w_branch.astype(BF16)
    w_out_b = w_out.astype(BF16)
    w_gate_up_b = w_gate_up.astype(BF16)
    w_down_b = w_down.astype(BF16)
    subln_g3 = subln_g.reshape(depth, 1, -1)
    conv_b3 = conv_b.reshape(depth, 1, -1)
    lru_lambda3 = lru_lambda.reshape(depth, 1, -1)
    cos_k, sin_k = _rope_tables(S)
    q_scale = HEAD_DIM ** -0.5 * math.log2(math.e)
    rope_q_t = (cos_k.T * q_scale, sin_k.T * q_scale)
    rope_k = (cos_k[None], sin_k[None])

    xf = x.reshape(T, D)
    xb = xf.astype(BF16)
    for l in range(depth):
        lam_init = 0.8 - 0.6 * math.exp(-0.3 * l)
        qt = _project_t(xb, wq_t, l, "proj_qt", tm=1024, tn=1024, rope=rope_q_t, seq=S)
        k = _project(xb, w_in_b, l, qk_w, qk_w, BF16, tm=1024, tn=2048, rope=rope_k, seq=S)
        vt = _project_t(xb, wv_t, l, "proj_vt", tm=1024, tn=2048)
        rest = _project(xb, w_in_b, l, 2 * qk_w + attn_w, 2 * d_rnn + 2 * D, F32, tm=1024, tn=2048)
        o_attn = _diff_attention(qt, k, vt, diff_lambda, subln_g3, l, lam_init,
                                 batch=B, seq=S, tq=512)
        h_rnn = _rglru(rest, conv_w, conv_b3, w_rg_b, b_rg_half, lru_lambda3, l,
                       batch=B, seq=S, width=d_rnn, tt=512)
        merged = _merge(o_attn, h_rnn, w_branch_b, rest, b_merge, l, tm=1024, tn=512)
        xf, xb = _outproj_ln(merged, w_out_b, xf, ln_g, ln_b, l, alpha, tm=512)
        xf, xb = _ffn_ln(xb, xf, w_gate_up_b, w_down_b, ln_g, ln_b, l, alpha, tm=512, tf=FFN_CHUNK)
    return xf.reshape(B, S, D)
```
